```python
import math
import jax, jax.numpy as jnp
from jax import lax
import numpy as np

D_MODEL = 1024
BATCH = 2
SEQ = 8192
DEPTH = 2
DEC_BATCH = 32
DEC_SEQ = 8
PAST_LEN = 16384
PAGE_SIZE = 128

SB_HEADS = 8
SB_HEAD_DIM = 64
SB_WIDTH = SB_HEADS * SB_HEAD_DIM
SB_SCALE = SB_HEAD_DIM ** -0.5
SB_BIAS_HI = -4.0
SB_BIAS_LO = -10.0
Q_BLOCK = 128
GDN_HEADS = 4
GDN_HEAD_DIM = 128
GDN_WIDTH = GDN_HEADS * GDN_HEAD_DIM
GDN_CONV = 4
GDN_CHUNK = 64
ML_HEADS = 8
ML_QK_DIM = 64
ML_V_DIM = 128
ML_QK_WIDTH = ML_HEADS * ML_QK_DIM
ML_V_WIDTH = ML_HEADS * ML_V_DIM
ML_CHUNK = 64
N_GROUPS = 4
EXPERTS_PER_GROUP = 4
N_EXPERTS = N_GROUPS * EXPERTS_PER_GROUP
TOP_K_IN_GROUP = 2
EXPERT_FF = 512
DN_ALPHA = (2 * DEPTH) ** 0.25
DN_BETA = (8 * DEPTH) ** -0.25
LN_EPS = 1e-5
RMS_EPS = 1e-6
L2_EPS = 1e-6

N_EVEN = (DEPTH + 1) // 2
N_ODD = DEPTH // 2
EVEN_IN = 3 * SB_WIDTH + 4 * GDN_WIDTH + 2 * GDN_HEADS
ODD_IN = 2 * ML_QK_WIDTH + 2 * ML_V_WIDTH + 2 * ML_HEADS

kernel_name = 'hybrid_sb_gdn_mlstm_hmoe_step'


def layer_norm(x, g, b):
    xf = x.astype(jnp.float32)
    mu = jnp.mean(xf, -1, keepdims=True)
    var = jnp.mean(jnp.square(xf - mu), -1, keepdims=True)
    return ((xf - mu) * lax.rsqrt(var + LN_EPS) * g + b).astype(x.dtype)


def rms_norm(x, g):
    xf = x.astype(jnp.float32)
    return xf * lax.rsqrt(jnp.mean(xf * xf, -1, keepdims=True) + RMS_EPS) * g


def l2_normalize(x):
    xf = x.astype(jnp.float32)
    return xf * lax.rsqrt(jnp.sum(xf * xf, -1, keepdims=True) + L2_EPS)


def to_chunks(a, L):
    B, T = a.shape[:2]
    a = a.reshape((B, T // L, L) + a.shape[2:])
    return a.transpose((1, 0, 3, 2) + tuple(range(4, a.ndim)))


def from_chunks(a):
    N, B, H, L = a.shape[:4]
    a = a.transpose((1, 0, 3, 2) + tuple(range(4, a.ndim)))
    return a.reshape((B, N * L, H) + a.shape[4:])


def sb_block(q_blk, q_pos, k, v, k_pos, bias):
    z = jnp.einsum('bqhd,bkhd->bhqk', q_blk, k) * SB_SCALE + bias[None, :, None, None]
    causal = k_pos[None, :] < q_pos[:, None]
    log_1mb = jnp.where(causal, jax.nn.log_sigmoid(-z), 0.0)
    tail = lax.cumsum(log_1mb, axis=3, reverse=True) - log_1mb
    w = jnp.where(causal, jnp.exp(jax.nn.log_sigmoid(z) + tail), 0.0)
    return jnp.einsum('bhqk,bkhd->bqhd', w, v)


def stick_breaking(q, k, v, q_start, bias):
    B, Tq, H, d = q.shape
    q, k, v = q.astype(jnp.float32), k.astype(jnp.float32), v.astype(jnp.float32)
    bias = bias.astype(jnp.float32)
    k_pos = jnp.arange(k.shape[1])
    q_pos = q_start + jnp.arange(Tq)
    if Tq > Q_BLOCK and Tq % Q_BLOCK == 0:
        nb = Tq // Q_BLOCK
        qb = q.reshape(B, nb, Q_BLOCK, H, d).transpose(1, 0, 2, 3, 4)
        pb = q_pos.reshape(nb, Q_BLOCK)
        ob = lax.map(lambda a: sb_block(a[0], a[1], k, v, k_pos, bias), (qb, pb))
        return ob.transpose(1, 0, 2, 3, 4).reshape(B, Tq, H, d)
    return sb_block(q, q_pos, k, v, k_pos, bias)


def causal_conv(x, buf, w):
    T = x.shape[1]
    xp = jnp.concatenate([buf, x], axis=1)
    y = sum(xp[:, i:i + T] * w[i] for i in range(GDN_CONV))
    return jax.nn.silu(y), xp[:, xp.shape[1] - (GDN_CONV - 1):]


def gated_delta_chunked(q, k, v, g, beta, S0):
    f32 = jnp.float32
    T = q.shape[1]
    C = math.gcd(T, GDN_CHUNK)
    qc, kc, vc = (to_chunks(a.astype(f32), C) for a in (q, k, v))
    gc = jnp.cumsum(to_chunks(g, C), axis=-1)
    bc = to_chunks(beta, C)
    idx = jnp.arange(C)
    lower = idx[:, None] >= idx[None, :]
    strict = idx[:, None] > idx[None, :]
    decay = jnp.exp(jnp.where(lower, gc[..., :, None] - gc[..., None, :], -jnp.inf))
    kb = kc * bc[..., None]
    A = jnp.where(strict, jnp.einsum('nbhid,nbhjd->nbhij', kb, kc) * decay, 0.0)
    rhs = jnp.concatenate([vc * bc[..., None], kb * jnp.exp(gc)[..., None]], axis=-1)
    sol = lax.linalg.triangular_solve(A + jnp.eye(C, dtype=f32), rhs, left_side=True, lower=True)
    dv = vc.shape[-1]
    u, w = sol[..., :dv], sol[..., dv:]

    def step(S, inp):
        q_, k_, u_, w_, g_, d_ = inp
        v_new = u_ - jnp.einsum('bhcd,bhde->bhce', w_, S)
        s = jnp.einsum('bhid,bhjd->bhij', q_, k_) * d_
        o = (jnp.einsum('bhcd,bhde->bhce', q_ * jnp.exp(g_)[..., None], S)
             + jnp.einsum('bhij,bhje->bhie', s, v_new))
        g_last = g_[..., -1]
        S = (jnp.exp(g_last)[..., None, None] * S
             + jnp.einsum('bhcd,bhce->bhde', k_ * jnp.exp(g_last[..., None] - g_)[..., None], v_new))
        return S, o

    S, o = lax.scan(step, S0.astype(f32), (qc, kc, u, w, gc, decay))
    return from_chunks(o), S


def even_mixer(x, q_start, past, conv_buf, S0, w_in, w_out, sb_bias, conv_w, a_log, dt_bias, norm_w):
    B, T, _ = x.shape
    proj = x @ w_in
    cuts = [SB_WIDTH, 2 * SB_WIDTH, 3 * SB_WIDTH, 3 * SB_WIDTH + 3 * GDN_WIDTH,
            3 * SB_WIDTH + 4 * GDN_WIDTH, 3 * SB_WIDTH + 4 * GDN_WIDTH + GDN_HEADS]
    q_sb, k_sb, v_sb, qkv_g, z, a, b = jnp.split(proj, cuts, axis=-1)
    q_sb = q_sb.reshape(B, T, SB_HEADS, SB_HEAD_DIM)
    k_sb = k_sb.reshape(B, T, SB_HEADS, SB_HEAD_DIM)
    v_sb = v_sb.reshape(B, T, SB_HEADS, SB_HEAD_DIM)
    if past is None:
        k_all, v_all = k_sb, v_sb
    else:
        pool_k, pool_v, page_table = past
        k_past = pool_k[page_table].reshape(B, -1, SB_HEADS, SB_HEAD_DIM)
        v_past = pool_v[page_table].reshape(B, -1, SB_HEADS, SB_HEAD_DIM)
        k_all = jnp.concatenate([k_past.astype(k_sb.dtype), k_sb], axis=1)
        v_all = jnp.concatenate([v_past.astype(v_sb.dtype), v_sb], axis=1)
    o_sb = stick_breaking(q_sb, k_all, v_all, q_start, sb_bias)
    qkv_c, new_buf = causal_conv(qkv_g, conv_buf.astype(qkv_g.dtype), conv_w)
    qg, kg, vg = jnp.split(qkv_c, 3, axis=-1)
    qg = l2_normalize(qg.reshape(B, T, GDN_HEADS, GDN_HEAD_DIM)) * GDN_HEAD_DIM ** -0.5
    kg = l2_normalize(kg.reshape(B, T, GDN_HEADS, GDN_HEAD_DIM))
    vg = vg.reshape(B, T, GDN_HEADS, GDN_HEAD_DIM)
    g = -jnp.exp(a_log.astype(jnp.float32)) * jax.nn.softplus(a.astype(jnp.float32) + dt_bias)
    beta = jax.nn.sigmoid(b.astype(jnp.float32))
    o_g, S = gated_delta_chunked(qg, kg, vg, g, beta, S0)
    o_g = rms_norm(o_g, norm_w) * jax.nn.silu(z.reshape(B, T, GDN_HEADS, GDN_HEAD_DIM).astype(jnp.float32))
    merged = jnp.concatenate([o_sb.reshape(B, T, SB_WIDTH), o_g.reshape(B, T, GDN_WIDTH)], axis=-1)
    y = merged.astype(x.dtype) @ w_out
    return y, k_sb, v_sb, new_buf, S


def mlstm_chunked(q, k, v, i_pre, f_pre, C0, n0, m0):
    f32 = jnp.float32
    T = q.shape[1]
    L = math.gcd(T, ML_CHUNK)
    qc, kc, vc = (to_chunks(a.astype(f32), L) for a in (q, k, v))
    ic = to_chunks(i_pre, L)
    bc = jnp.cumsum(to_chunks(jax.nn.log_sigmoid(f_pre), L), axis=-1)
    idx = jnp.arange(L)
    causal = idx[:, None] >= idx[None, :]
    Dm = jnp.where(causal, bc[..., :, None] - bc[..., None, :] + ic[..., None, :], -jnp.inf)

    def step(carry, inp):
        C, n, m = carry
        q_, k_, v_, b_, D_ = inp
        inter = b_ + m[..., None]
        m_t = jnp.maximum(inter, jnp.max(D_, axis=-1))
        w_intra = jnp.exp(D_ - m_t[..., None])
        w_state = jnp.exp(inter - m_t)
        s = jnp.einsum('bhid,bhjd->bhij', q_, k_) * w_intra
        num = (w_state[..., None] * jnp.einsum('bhld,bhde->bhle', q_, C)
               + jnp.einsum('bhij,bhje->bhie', s, v_))
        den = w_state * jnp.einsum('bhld,bhd->bhl', q_, n) + jnp.sum(s, -1)
        h = num / jnp.maximum(jnp.abs(den), jnp.exp(-m_t))[..., None]
        m_new = m_t[..., -1]
        a_state = jnp.exp(b_[..., -1] + m - m_new)
        wk = jnp.exp(D_[..., -1, :] - m_new[..., None])
        C = a_state[..., None, None] * C + jnp.einsum('bhl,bhld,bhle->bhde', wk, k_, v_)
        n = a_state[..., None] * n + jnp.einsum('bhl,bhld->bhd', wk, k_)
        return (C, n, m_new), h

    (C, n, m), h = lax.scan(step, (C0.astype(f32), n0.astype(f32), m0.astype(f32)), (qc, kc, vc, bc, Dm))
    return from_chunks(h), C, n, m


def odd_mixer(x, C0, n0, m0, w_in, w_out, i_bias, f_bias, norm_w):
    B, T, _ = x.shape
    proj = x @ w_in
    cuts = [ML_QK_WIDTH, 2 * ML_QK_WIDTH, 2 * ML_QK_WIDTH + ML_V_WIDTH,
            2 * ML_QK_WIDTH + 2 * ML_V_WIDTH, 2 * ML_QK_WIDTH + 2 * ML_V_WIDTH + ML_HEADS]
    q, k, v, o, ig, fg = jnp.split(proj, cuts, axis=-1)
    q = q.reshape(B, T, ML_HEADS, ML_QK_DIM)
    k = k.reshape(B, T, ML_HEADS, ML_QK_DIM) * ML_QK_DIM ** -0.5
    v = v.reshape(B, T, ML_HEADS, ML_V_DIM)
    i_pre = ig.astype(jnp.float32) + i_bias
    f_pre = fg.astype(jnp.float32) + f_bias
    h, C, n, m = mlstm_chunked(q, k, v, i_pre, f_pre, C0, n0, m0)
    h = rms_norm(h, norm_w) * jax.nn.sigmoid(o.reshape(B, T, ML_HEADS, ML_V_DIM).astype(jnp.float32))
    y = h.reshape(B, T, ML_V_WIDTH).astype(x.dtype) @ w_out
    return y, C, n, m


def hier_moe(x, w_group, b_group, w_fine, b_fine, w_gate, w_up, w_down):
    B, T, D = x.shape
    xt = x.reshape(B * T, D)
    g_prob = jax.nn.softmax((xt @ w_group).astype(jnp.float32) + b_group, axis=-1)
    g_gate, g_idx = lax.top_k(g_prob, 1)
    onehot_g = jax.nn.one_hot(g_idx[:, 0], N_GROUPS, dtype=jnp.float32)
    f_logits = jnp.einsum('nd,dge->nge', xt, w_fine).astype(jnp.float32) + b_fine
    f_prob = jax.nn.softmax(jnp.einsum('ng,nge->ne', onehot_g, f_logits), axis=-1)
    top_w, top_i = lax.top_k(f_prob, TOP_K_IN_GROUP)
    top_w = top_w / jnp.sum(top_w, -1, keepdims=True)
    fine = jnp.sum(jax.nn.one_hot(top_i, EXPERTS_PER_GROUP, dtype=jnp.float32) * top_w[..., None], axis=1)
    gates = (onehot_g[:, :, None] * (g_gate[:, :, None] * fine[:, None, :])).reshape(B * T, N_EXPERTS)
    h = jax.nn.silu(jnp.einsum('nd,edf->nef', xt, w_gate)) * jnp.einsum('nd,edf->nef', xt, w_up)
    y = jnp.einsum('nef,efd->nd', h * gates[:, :, None].astype(h.dtype), w_down)
    return y.reshape(B, T, D)


def run_trunk(x, q_start, pasts, conv_bufs, gdn_states, ml_states, p):
    ks, vs, bufs, Ss, Cs, ns, ms = [], [], [], [], [], [], []
    for layer in range(DEPTH):
        j = layer // 2
        if layer % 2 == 0:
            h, k_r, v_r, buf, S = even_mixer(
                x, q_start, pasts[j], conv_bufs[j], gdn_states[j],
                p['even_w_in'][j], p['even_w_out'][j], p['sb_bias'][j], p['gdn_conv_w'][j],
                p['gdn_a_log'][j], p['gdn_dt_bias'][j], p['gdn_norm_w'][j])
            ks.append(k_r)
            vs.append(v_r)
            bufs.append(buf)
            Ss.append(S)
        else:
            C0, n0, m0 = ml_states[j]
            h, C, n, m = odd_mixer(
                x, C0, n0, m0, p['odd_w_in'][j], p['odd_w_out'][j],
                p['mlstm_i_bias'][j], p['mlstm_f_bias'][j], p['mlstm_norm_w'][j])
            Cs.append(C)
            ns.append(n)
            ms.append(m)
        x = layer_norm(DN_ALPHA * x + h, p['ln_mix_g'][layer], p['ln_mix_b'][layer])
        f = hier_moe(x, p['moe_w_group'][layer], p['moe_b_group'][layer], p['moe_w_fine'][layer],
                     p['moe_b_fine'][layer], p['moe_w_gate'][layer], p['moe_w_up'][layer], p['moe_w_down'][layer])
        x = layer_norm(DN_ALPHA * x + f, p['ln_ffn_g'][layer], p['ln_ffn_b'][layer])
    return (x, jnp.stack(ks), jnp.stack(vs), jnp.stack(bufs), jnp.stack(Ss),
            jnp.stack(Cs), jnp.stack(ns), jnp.stack(ms))


def setup_inputs(seed: int = 0) -> dict:
    key = jax.random.key(seed)
    keys = iter(jax.random.split(key, 48))
    f32 = jnp.float32

    def normal(shape, scale):
        return scale * jax.random.normal(next(keys), shape, f32)

    n_pages = PAST_LEN // PAGE_SIZE
    n_used = DEC_BATCH * n_pages
    n_pool = n_used + max(1, n_used // 4)
    x_prompt = normal((BATCH, SEQ, D_MODEL), 1.0)
    x_sample = normal((DEC_BATCH, DEC_SEQ, D_MODEL), 1.0)
    cache_k = normal((N_EVEN, n_pool, PAGE_SIZE, SB_HEADS, SB_HEAD_DIM), 1.0)
    cache_v = normal((N_EVEN, n_pool, PAGE_SIZE, SB_HEADS, SB_HEAD_DIM), 1.0)
    state_conv = normal((N_EVEN, DEC_BATCH, GDN_CONV - 1, 3 * GDN_WIDTH), 1.0)
    state_gdn = normal((N_EVEN, DEC_BATCH, GDN_HEADS, GDN_HEAD_DIM, GDN_HEAD_DIM), 0.1)
    state_mlstm_C = normal((N_ODD, DEC_BATCH, ML_HEADS, ML_QK_DIM, ML_V_DIM), 0.5)
    state_mlstm_n = normal((N_ODD, DEC_BATCH, ML_HEADS, ML_QK_DIM), 0.5)
    state_mlstm_m = normal((N_ODD, DEC_BATCH, ML_HEADS), 1.0)
    page_table = jax.random.permutation(next(keys), n_pool)[:n_used].reshape(DEC_BATCH, n_pages).astype(jnp.int32)

    even_w_in = normal((N_EVEN, D_MODEL, EVEN_IN), D_MODEL ** -0.5)
    even_w_out = normal((N_EVEN, SB_WIDTH + GDN_WIDTH, D_MODEL), DN_BETA * (SB_WIDTH + GDN_WIDTH) ** -0.5)
    sb_bias = (jnp.linspace(SB_BIAS_HI, SB_BIAS_LO, SB_HEADS, dtype=f32)[None, :]
               + normal((N_EVEN, SB_HEADS), 0.1))
    gdn_conv_w = normal((N_EVEN, GDN_CONV, 3 * GDN_WIDTH), GDN_CONV ** -0.5)
    gdn_a_log = jnp.log(jax.random.uniform(next(keys), (N_EVEN, GDN_HEADS), f32, 1.0, 16.0))
    dt = jnp.exp(jax.random.uniform(next(keys), (N_EVEN, GDN_HEADS), f32, math.log(1e-3), math.log(1e-1)))
    gdn_dt_bias = dt + jnp.log(-jnp.expm1(-dt))
    gdn_norm_w = 1.0 + normal((N_EVEN, GDN_HEAD_DIM), 0.01)
    odd_w_in = normal((N_ODD, D_MODEL, ODD_IN), D_MODEL ** -0.5)
    odd_w_out = normal((N_ODD, ML_V_WIDTH, D_MODEL), DN_BETA * ML_V_WIDTH ** -0.5)
    mlstm_i_bias = normal((N_ODD, ML_HEADS), 0.1)
    mlstm_f_bias = 3.0 + jnp.linspace(0.0, 3.0, ML_HEADS, dtype=f32)[None, :] + normal((N_ODD, ML_HEADS), 0.1)
    mlstm_norm_w = 1.0 + normal((N_ODD, ML_V_DIM), 0.01)
    ln_mix_g = 1.0 + normal((DEPTH, D_MODEL), 0.01)
    ln_mix_b = normal((DEPTH, D_MODEL), 0.01)
    ln_ffn_g = 1.0 + normal((DEPTH, D_MODEL), 0.01)
    ln_ffn_b = normal((DEPTH, D_MODEL), 0.01)
    moe_w_group = normal((DEPTH, D_MODEL, N_GROUPS), D_MODEL ** -0.5)
    moe_b_group = normal((DEPTH, N_GROUPS), 0.01)
    moe_w_fine = normal((DEPTH, D_MODEL, N_GROUPS, EXPERTS_PER_GROUP), D_MODEL ** -0.5)
    moe_b_fine = normal((DEPTH, N_GROUPS, EXPERTS_PER_GROUP), 0.01)
    moe_w_gate = normal((DEPTH, N_EXPERTS, D_MODEL, EXPERT_FF), D_MODEL ** -0.5)
    moe_w_up = normal((DEPTH, N_EXPERTS, D_MODEL, EXPERT_FF), D_MODEL ** -0.5)
    moe_w_down = normal((DEPTH, N_EXPERTS, EXPERT_FF, D_MODEL), DN_BETA * EXPERT_FF ** -0.5)
    return {'x_prompt': x_prompt, 'x_sample': x_sample, 'cache_k': cache_k, 'cache_v': cache_v,
            'state_conv': state_conv, 'state_gdn': state_gdn, 'state_mlstm_C': state_mlstm_C,
            'state_mlstm_n': state_mlstm_n, 'state_mlstm_m': state_mlstm_m, 'page_table': page_table,
            'even_w_in': even_w_in, 'even_w_out': even_w_out, 'sb_bias': sb_bias, 'gdn_conv_w': gdn_conv_w,
            'gdn_a_log': gdn_a_log, 'gdn_dt_bias': gdn_dt_bias, 'gdn_norm_w': gdn_norm_w,
            'odd_w_in': odd_w_in, 'odd_w_out': odd_w_out, 'mlstm_i_bias': mlstm_i_bias,
            'mlstm_f_bias': mlstm_f_bias, 'mlstm_norm_w': mlstm_norm_w,
            'ln_mix_g': ln_mix_g, 'ln_mix_b': ln_mix_b, 'ln_ffn_g': ln_ffn_g, 'ln_ffn_b': ln_ffn_b,
            'moe_w_group': moe_w_group, 'moe_b_group': moe_b_group, 'moe_w_fine': moe_w_fine,
            'moe_b_fine': moe_b_fine, 'moe_w_gate': moe_w_gate, 'moe_w_up': moe_w_up, 'moe_w_down': moe_w_down}


def reference(x_prompt, x_sample, cache_k, cache_v, state_conv, state_gdn, state_mlstm_C, state_mlstm_n,
              state_mlstm_m, page_table, even_w_in, even_w_out, sb_bias, gdn_conv_w, gdn_a_log, gdn_dt_bias,
              gdn_norm_w, odd_w_in, odd_w_out, mlstm_i_bias, mlstm_f_bias, mlstm_norm_w, ln_mix_g, ln_mix_b,
              ln_ffn_g, ln_ffn_b, moe_w_group, moe_b_group, moe_w_fine, moe_b_fine, moe_w_gate, moe_w_up,
              moe_w_down):
    p = {'even_w_in': even_w_in, 'even_w_out': even_w_out, 'sb_bias': sb_bias, 'gdn_conv_w': gdn_conv_w,
         'gdn_a_log': gdn_a_log, 'gdn_dt_bias': gdn_dt_bias, 'gdn_norm_w': gdn_norm_w,
         'odd_w_in': odd_w_in, 'odd_w_out': odd_w_out,
         'mlstm_i_bias': mlstm_i_bias, 'mlstm_f_bias': mlstm_f_bias, 'mlstm_norm_w': mlstm_norm_w,
         'ln_mix_g': ln_mix_g, 'ln_mix_b': ln_mix_b, 'ln_ffn_g': ln_ffn_g, 'ln_ffn_b': ln_ffn_b,
         'moe_w_group': moe_w_group, 'moe_b_group': moe_b_group, 'moe_w_fine': moe_w_fine,
         'moe_b_fine': moe_b_fine, 'moe_w_gate': moe_w_gate, 'moe_w_up': moe_w_up, 'moe_w_down': moe_w_down}
    f32 = jnp.float32
    B = x_prompt.shape[0]
    zero_buf = jnp.zeros((B, GDN_CONV - 1, 3 * GDN_WIDTH), x_prompt.dtype)
    zero_S = jnp.zeros((B, GDN_HEADS, GDN_HEAD_DIM, GDN_HEAD_DIM), f32)
    zero_ml = (jnp.zeros((B, ML_HEADS, ML_QK_DIM, ML_V_DIM), f32),
               jnp.zeros((B, ML_HEADS, ML_QK_DIM), f32), jnp.zeros((B, ML_HEADS), f32))
    (y_prompt, k_prompt, v_prompt, conv_prompt, gdn_prompt,
     mC_prompt, mn_prompt, mm_prompt) = run_trunk(
        x_prompt, 0, [None] * N_EVEN, [zero_buf] * N_EVEN, [zero_S] * N_EVEN, [zero_ml] * N_ODD, p)
    past_len = page_table.shape[1] * cache_k.shape[2]
    pasts = [(cache_k[j], cache_v[j], page_table) for j in range(N_EVEN)]
    ml_states = [(state_mlstm_C[j], state_mlstm_n[j], state_mlstm_m[j]) for j in range(N_ODD)]
    (y_sample, k_sample, v_sample, conv_sample, gdn_sample,
     mC_sample, mn_sample, mm_sample) = run_trunk(
        x_sample, past_len, pasts, [state_conv[j] for j in range(N_EVEN)],
        [state_gdn[j] for j in range(N_EVEN)], ml_states, p)
    return (y_prompt, y_sample, k_prompt, v_prompt, k_sample, v_sample, conv_prompt, conv_sample,
            gdn_prompt, gdn_sample, mC_prompt, mC_sample, mn_prompt, mn_sample, mm_prompt, mm_sample)
```

```python
import functools
import math

import jax
import jax.numpy as jnp
from jax import lax
from jax.experimental import pallas as pl
from jax.experimental.pallas import tpu as pltpu

F32 = jnp.float32
BF16 = jnp.bfloat16

D_MODEL = 1024
DEPTH = 2
PAGE_SIZE = 128
SB_HEADS = 8
SB_HEAD_DIM = 64
SB_WIDTH = SB_HEADS * SB_HEAD_DIM
SB_SCALE = SB_HEAD_DIM ** -0.5
GDN_HEADS = 4
GDN_HEAD_DIM = 128
GDN_WIDTH = GDN_HEADS * GDN_HEAD_DIM
GDN_CONV = 4
ML_HEADS = 8
ML_QK_DIM = 64
ML_V_DIM = 128
ML_QK_WIDTH = ML_HEADS * ML_QK_DIM
ML_V_WIDTH = ML_HEADS * ML_V_DIM
N_GROUPS = 4
EXPERTS_PER_GROUP = 4
N_EXPERTS = N_GROUPS * EXPERTS_PER_GROUP
EXPERT_FF = 512
DN_ALPHA = (2 * DEPTH) ** 0.25
LN_EPS = 1e-5
RMS_EPS = 1e-6
L2_EPS = 1e-6

LANES = 128
CHUNK = 64
NEG_BIG = -1e30
VMEM_LIMIT = 56 * 1024 * 1024


def _cparams(sem):
    return pltpu.CompilerParams(dimension_semantics=sem, vmem_limit_bytes=VMEM_LIMIT)


def _mm(a, b):
    return jnp.dot(a, b, preferred_element_type=F32)


def _mm_nt(a, b):
    return lax.dot_general(a, b, (((1,), (1,)), ((), ())), preferred_element_type=F32)


def _mm_tn(a, b):
    return lax.dot_general(a, b, (((0,), (0,)), ((), ())), preferred_element_type=F32)


def _split3(x):
    x1 = x.astype(BF16)
    r1 = x - x1.astype(F32)
    x2 = r1.astype(BF16)
    x3 = (r1 - x2.astype(F32)).astype(BF16)
    return x1, x2, x3


def _mm_exact_lhs(a_bf16, b):
    b1, b2, b3 = _split3(b)
    return _mm(a_bf16, b1) + _mm(a_bf16, b2) + _mm(a_bf16, b3)


def _mm3(a, b):
    a1, a2, _ = _split3(a)
    b1, b2, _ = _split3(b)
    return _mm(a1, b1) + (_mm(a1, b2) + _mm(a2, b1))


def _softplus(x):
    return jnp.maximum(x, 0.0) + jnp.log1p(jnp.exp(-jnp.abs(x)))


def _sigmoid(x):
    return jax.nn.sigmoid(x)


def _iota(shape, dim):
    return lax.broadcasted_iota(jnp.int32, shape, dim)


def _proj_kernel(x_ref, w_ref, *out_refs, widths):
    x = x_ref[...].astype(BF16)
    off = 0
    for o_ref, wd in zip(out_refs, widths):
        o_ref[...] = _mm(x, w_ref[:, off:off + wd])
        off += wd


def _project(x, w_bf16, widths, tm):
    n, d = x.shape
    tm = min(tm, n)
    total = sum(widths)
    return pl.pallas_call(
        functools.partial(_proj_kernel, widths=tuple(widths)),
        grid=(n // tm,),
        in_specs=[pl.BlockSpec((tm, d), lambda i: (i, 0)),
                  pl.BlockSpec((d, total), lambda i: (0, 0))],
        out_specs=[pl.BlockSpec((tm, wd), lambda i: (i, 0)) for wd in widths],
        out_shape=[jax.ShapeDtypeStruct((n, wd), F32) for wd in widths],
        compiler_params=_cparams(("parallel",)),
        name="proj",
    )(x, w_bf16)


def _layer_norm_rows(r, g, b):
    mu = jnp.mean(r, axis=-1, keepdims=True)
    c = r - mu
    var = jnp.mean(c * c, axis=-1, keepdims=True)
    return c * lax.rsqrt(var + LN_EPS) * g + b


def _outproj_ln_kernel(m_ref, w_ref, x_ref, g_ref, b_ref, o_ref):
    y = _mm(m_ref[...].astype(BF16), w_ref[...])
    o_ref[...] = _layer_norm_rows(DN_ALPHA * x_ref[...] + y, g_ref[...], b_ref[...])


def _outproj_ln(merged, w_bf16, x, g, b, tm):
    n, k = merged.shape
    d = x.shape[1]
    tm = min(tm, n)
    return pl.pallas_call(
        _outproj_ln_kernel,
        grid=(n // tm,),
        in_specs=[pl.BlockSpec((tm, k), lambda i: (i, 0)),
                  pl.BlockSpec((k, d), lambda i: (0, 0)),
                  pl.BlockSpec((tm, d), lambda i: (i, 0)),
                  pl.BlockSpec((1, d), lambda i: (0, 0)),
                  pl.BlockSpec((1, d), lambda i: (0, 0))],
        out_specs=pl.BlockSpec((tm, d), lambda i: (i, 0)),
        out_shape=jax.ShapeDtypeStruct((n, d), F32),
        compiler_params=_cparams(("parallel",)),
        name="outproj_ln",
    )(merged, w_bf16, x, g.reshape(1, d), b.reshape(1, d))


def _sb_tile(z, masked, causal, u_bf16, r_prev):
    l = -_softplus(z)
    lm = jnp.where(causal, l, 0.0) if masked else l
    tail = _mm(lm.astype(BF16), u_bf16) + r_prev
    w = jnp.exp(z + l + tail)
    if masked:
        w = jnp.where(causal, w, 0.0)
    return w, jnp.sum(lm, axis=-1, keepdims=True)


def _sb_prompt_kernel(qi_ref, kb_ref, bias_ref, u_ref, q_ref, k_ref, v_ref, o_ref, acc_ref, r_ref,
                      *, tq, tk):
    hp = pl.program_id(1)
    p = pl.program_id(2)
    q0 = qi_ref[p] * tq
    k0 = kb_ref[p] * tk

    @pl.when(k0 + tk >= q0 + tq)
    def _():
        acc_ref[...] = jnp.zeros_like(acc_ref)
        r_ref[...] = jnp.zeros_like(r_ref)

    def body(masked):
        lane = _iota((1, LANES), 1)
        q2 = q_ref[0] * SB_SCALE
        k2 = k_ref[0].astype(BF16)
        v2 = v_ref[0]
        u = u_ref[...]
        causal = None
        if masked:
            causal = (_iota((tq, tk), 1) + k0) < (_iota((tq, tk), 0) + q0)
        upd = jnp.zeros((tq, LANES), F32)
        for e in range(2):
            in_head = (lane >= SB_HEAD_DIM * e) & (lane < SB_HEAD_DIM * (e + 1))
            qh = jnp.where(in_head, q2, 0.0).astype(BF16)
            vh = jnp.where(in_head, v2, 0.0).astype(BF16)
            z = _mm_nt(qh, k2) + bias_ref[2 * hp + e]
            w, lsum = _sb_tile(z, masked, causal, u, r_ref[e])
            upd = upd + _mm(w.astype(BF16), vh)
            r_ref[e] = r_ref[e] + lsum
        acc_ref[...] += upd

    needs_mask = k0 + tk > q0

    @pl.when(needs_mask)
    def _():
        body(True)

    @pl.when(jnp.logical_not(needs_mask))
    def _():
        body(False)

    @pl.when(k0 == 0)
    def _():
        o_ref[0] = acc_ref[...]


def _upper_ones(n):
    return (jnp.arange(n)[:, None] > jnp.arange(n)[None, :]).astype(BF16)


def _sb_prompt(q, k, v, bias, tq=256, tk=256):
    b, t, _ = q.shape
    tq, tk = min(tq, t), min(tk, t)
    nq = t // tq
    qi_list, kb_list = [], []
    for i in range(nq):
        top = (i + 1) * tq // tk - 1
        for kb in range(top, -1, -1):
            qi_list.append(i)
            kb_list.append(kb)
    qi = jnp.asarray(qi_list, jnp.int32)
    kb = jnp.asarray(kb_list, jnp.int32)
    grid_spec = pltpu.PrefetchScalarGridSpec(
        num_scalar_prefetch=2,
        grid=(b, SB_HEADS // 2, len(qi_list)),
        in_specs=[pl.BlockSpec(memory_space=pltpu.SMEM),
                  pl.BlockSpec((tk, tk), lambda bb, hp, p, qi, kb: (0, 0)),
                  pl.BlockSpec((1, tq, LANES), lambda bb, hp, p, qi, kb: (bb, qi[p], hp)),
                  pl.BlockSpec((1, tk, LANES), lambda bb, hp, p, qi, kb: (bb, kb[p], hp)),
                  pl.BlockSpec((1, tk, LANES), lambda bb, hp, p, qi, kb: (bb, kb[p], hp))],
        out_specs=pl.BlockSpec((1, tq, LANES), lambda bb, hp, p, qi, kb: (bb, qi[p], hp)),
        scratch_shapes=[pltpu.VMEM((tq, LANES), F32), pltpu.VMEM((2, tq, 1), F32)],
    )
    return pl.pallas_call(
        functools.partial(_sb_prompt_kernel, tq=tq, tk=tk),
        grid_spec=grid_spec,
        out_shape=jax.ShapeDtypeStruct((b, t, SB_WIDTH), F32),
        compiler_params=_cparams(("parallel", "parallel", "arbitrary")),
        name="sb_prompt",
    )(qi, kb, bias, _upper_ones(tk), q, k, v)


PAGES_PER_STEP = 8


def _sb_sample_kernel(pt_ref, bias_ref, u_ref, q_ref, kn_ref, vn_ref, *rest, t_new):
    k_refs = rest[:PAGES_PER_STEP]
    v_refs = rest[PAGES_PER_STEP:2 * PAGES_PER_STEP]
    o_ref, qf_ref, bcol_ref, acc_ref, r_ref = rest[2 * PAGES_PER_STEP:]
    rows = SB_HEADS * t_new
    j = pl.program_id(1)

    def process(k_page, v_page, masked):
        kp = k_page.astype(BF16)
        vp = v_page.astype(BF16)
        z = _mm_nt(qf_ref[...], kp) + bcol_ref[...]
        causal = None
        if masked:
            t_of_row = _iota((rows, PAGE_SIZE), 0) % t_new
            causal = _iota((rows, PAGE_SIZE), 1) < t_of_row
        w, lsum = _sb_tile(z, masked, causal, u_ref[...], r_ref[...])
        acc_ref[...] += _mm(w.astype(BF16), vp)
        r_ref[...] = r_ref[...] + lsum

    @pl.when(j == 0)
    def _():
        q = q_ref[0] * SB_SCALE
        qt = jnp.concatenate([q] * SB_HEADS, axis=0)
        head_of_row = _iota((rows, SB_WIDTH), 0) // t_new
        head_of_lane = _iota((rows, SB_WIDTH), 1) // SB_HEAD_DIM
        qf_ref[...] = jnp.where(head_of_row == head_of_lane, qt, 0.0).astype(BF16)
        hrow = _iota((rows, 1), 0) // t_new
        bcol = jnp.zeros((rows, 1), F32)
        for h in range(SB_HEADS):
            bcol = jnp.where(hrow == h, bias_ref[h], bcol)
        bcol_ref[...] = bcol
        acc_ref[...] = jnp.zeros_like(acc_ref)
        r_ref[...] = jnp.zeros_like(r_ref)
        process(kn_ref[0], vn_ref[0], True)

    for r in range(PAGES_PER_STEP):
        process(k_refs[r][0], v_refs[r][0], False)

    @pl.when(j == pl.num_programs(1) - 1)
    def _():
        lane_head = _iota((t_new, SB_WIDTH), 1) // SB_HEAD_DIM
        o = jnp.zeros((t_new, SB_WIDTH), F32)
        for h in range(SB_HEADS):
            o = o + jnp.where(lane_head == h, acc_ref[h * t_new:(h + 1) * t_new, :], 0.0)
        o_ref[0] = o


def _sb_sample(q, k_new, v_new, pool_k, pool_v, page_table, bias):
    b, t_new, _ = q.shape
    n_pages = page_table.shape[1]
    steps = n_pages // PAGES_PER_STEP
    pad = ((0, 0), (0, PAGE_SIZE - t_new), (0, 0))
    kn = jnp.pad(k_new, pad)
    vn = jnp.pad(v_new, pad)
    rows = SB_HEADS * t_new

    def page_spec(r):
        def imap(bb, j, pt):
            return (pt[bb, n_pages - 1 - (j * PAGES_PER_STEP + r)], 0, 0)
        return pl.BlockSpec((1, PAGE_SIZE, SB_WIDTH), imap)

    seq_spec = lambda rws: pl.BlockSpec((1, rws, SB_WIDTH), lambda bb, j, pt: (bb, 0, 0))
    grid_spec = pltpu.PrefetchScalarGridSpec(
        num_scalar_prefetch=1,
        grid=(b, steps),
        in_specs=([pl.BlockSpec(memory_space=pltpu.SMEM),
                   pl.BlockSpec((PAGE_SIZE, PAGE_SIZE), lambda bb, j, pt: (0, 0)),
                   seq_spec(t_new), seq_spec(PAGE_SIZE), seq_spec(PAGE_SIZE)]
                  + [page_spec(r) for r in range(PAGES_PER_STEP)]
                  + [page_spec(r) for r in range(PAGES_PER_STEP)]),
        out_specs=seq_spec(t_new),
        scratch_shapes=[pltpu.VMEM((rows, SB_WIDTH), BF16), pltpu.VMEM((rows, 1), F32),
                        pltpu.VMEM((rows, SB_WIDTH), F32), pltpu.VMEM((rows, 1), F32)],
    )
    return pl.pallas_call(
        functools.partial(_sb_sample_kernel, t_new=t_new),
        grid_spec=grid_spec,
        out_shape=jax.ShapeDtypeStruct((b, t_new, SB_WIDTH), F32),
        compiler_params=_cparams(("parallel", "arbitrary")),
        name="sb_sample",
    )(page_table, bias, _upper_ones(PAGE_SIZE), q, kn, vn,
      *([pool_k] * PAGES_PER_STEP), *([pool_v] * PAGES_PER_STEP))


def _chunk_masks():
    i = _iota((CHUNK, CHUNK), 0)
    j = _iota((CHUNK, CHUNK), 1)
    return i >= j, i > j


def _gdn_prep_kernel(alog_ref, dtb_ref, qr_ref, kr_ref, vr_ref, qp_ref, kp_ref, vp_ref,
                     qs_ref, ks_ref, vs_ref, wq_ref, wk_ref, wv_ref, ab_ref,
                     u_ref, w_ref, qg_ref, kd_ref, sd_ref, eg_ref, xbuf_ref, *, tc, t_valid):
    h = pl.program_id(1)
    c = pl.program_id(2)

    def conv_silu(x_ref, prev_ref, state_ref, wt_ref):
        first = (c == 0).astype(F32)
        xbuf_ref[0:8, :] = first * state_ref[0] + (1.0 - first) * prev_ref[0]
        xbuf_ref[8:, :] = x_ref[0]
        y = jnp.zeros((tc, LANES), F32)
        for i in range(GDN_CONV):
            y = y + xbuf_ref[5 + i:5 + i + tc, :] * wt_ref[i:i + 1, :]
        return y * _sigmoid(y)

    qc = conv_silu(qr_ref, qp_ref, qs_ref, wq_ref)
    kc = conv_silu(kr_ref, kp_ref, ks_ref, wk_ref)
    vc = conv_silu(vr_ref, vp_ref, vs_ref, wv_ref)
    q = qc * lax.rsqrt(jnp.sum(qc * qc, axis=-1, keepdims=True) + L2_EPS) * GDN_HEAD_DIM ** -0.5
    k = kc * lax.rsqrt(jnp.sum(kc * kc, axis=-1, keepdims=True) + L2_EPS)

    ab = ab_ref[0]
    lane = _iota((tc, LANES), 1)
    a_b = jnp.broadcast_to(jnp.sum(jnp.where(lane == h, ab, 0.0), axis=-1, keepdims=True), (tc, LANES))
    b_b = jnp.broadcast_to(jnp.sum(jnp.where(lane == GDN_HEADS + h, ab, 0.0), axis=-1, keepdims=True),
                           (tc, LANES))
    neg_a = -jnp.exp(jnp.full((1, LANES), alog_ref[h], F32))
    g_b = neg_a * _softplus(a_b + dtb_ref[h])
    beta_b = _sigmoid(b_b)
    valid = (_iota((tc, LANES), 0) + c * tc) < t_valid
    g_b = jnp.where(valid, g_b, 0.0)
    beta_b = jnp.where(valid, beta_b, 0.0)

    lower, strict = _chunk_masks()
    t1 = lower.astype(BF16)
    t2 = (_iota((CHUNK, CHUNK), 0) > _iota((CHUNK, CHUNK), 1)).astype(F32)
    eye = (_iota((CHUNK, CHUNK), 0) == _iota((CHUNK, CHUNK), 1)).astype(F32)

    for n in range(tc // CHUNK):
        sl = slice(n * CHUNK, (n + 1) * CHUNK)
        gch = g_b[sl]
        gc_b = _mm_exact_lhs(t1, gch)
        dg = _mm_exact_lhs(t1, gch[:, :CHUNK] * t2)
        decay = jnp.where(lower, jnp.exp(jnp.where(lower, dg, 0.0)), 0.0)
        kch, qch, vch, bch = k[sl], q[sl], vc[sl], beta_b[sl]
        kbeta = kch * bch
        kb16 = kch.astype(BF16)
        a = jnp.where(strict, _mm_nt(kbeta.astype(BF16), kb16) * decay, 0.0)
        tinv = eye - a
        pw = a
        for _ in range(int(math.log2(CHUNK)) - 1):
            pw = _mm3(pw, pw)
            tinv = tinv + _mm3(tinv, pw)
        rhs = jnp.concatenate([vch * bch, kbeta * jnp.exp(gc_b)], axis=1)
        sol = _mm3(tinv, rhs)
        u_ref[0, 0, sl, :] = sol[:, :GDN_HEAD_DIM]
        w_ref[0, 0, sl, :] = sol[:, GDN_HEAD_DIM:]
        sd_ref[0, 0, sl, :] = _mm_nt(qch.astype(BF16), kb16) * decay
        qg_ref[0, 0, sl, :] = qch * jnp.exp(gc_b)
        g_last = gc_b[CHUNK - 1:CHUNK, :]
        kd_ref[0, 0, sl, :] = kch * jnp.exp(g_last - gc_b)
        eg_ref[0, 0, n:n + 1, :] = jnp.exp(g_last)


def _gdn_scan_kernel(u_ref, w_ref, qg_ref, kd_ref, sd_ref, eg_ref, z_ref, nw_ref, s0_ref,
                     o_ref, sout_ref, s_ref, *, tc):
    c = pl.program_id(2)

    @pl.when(c == 0)
    def _():
        s_ref[...] = s0_ref[0, 0]

    s = s_ref[...]
    for n in range(tc // CHUNK):
        sl = slice(n * CHUNK, (n + 1) * CHUNK)
        s16 = s.astype(BF16)
        v_new = u_ref[0, 0, sl, :] - _mm(w_ref[0, 0, sl, :].astype(BF16), s16)
        vn16 = v_new.astype(BF16)
        o = _mm(qg_ref[0, 0, sl, :].astype(BF16), s16) + _mm(sd_ref[0, 0, sl, :].astype(BF16), vn16)
        s = eg_ref[0, 0, n:n + 1, :] * s + _mm_tn(kd_ref[0, 0, sl, :].astype(BF16), vn16)
        zg = z_ref[0, sl, :]
        on = o * lax.rsqrt(jnp.mean(o * o, axis=-1, keepdims=True) + RMS_EPS) * nw_ref[...]
        o_ref[0, sl, :] = on * (zg * _sigmoid(zg))
    s_ref[...] = s

    @pl.when(c == pl.num_programs(2) - 1)
    def _():
        sout_ref[0, 0] = s


def _gdn(qkv, z, ab, conv_prev8, s0, conv_w, a_log, dt_bias, norm_w, t_valid):
    b, t, _ = qkv.shape
    tc = min(512, t)
    nb = t // tc
    nch = t // CHUNK
    hd = GDN_HEAD_DIM
    col = lambda off: pl.BlockSpec((1, tc, hd), lambda bb, h, c: (bb, c, h + off))
    prev = lambda off: pl.BlockSpec((1, 8, hd), lambda bb, h, c: (bb, jnp.maximum(c * (tc // 8) - 1, 0), h + off))
    state = lambda off: pl.BlockSpec((1, 8, hd), lambda bb, h, c: (bb, 0, h + off))
    wspec = lambda off: pl.BlockSpec((GDN_CONV, hd), lambda bb, h, c: (0, h + off))
    per_head = lambda width: pl.BlockSpec((1, 1, tc, width), lambda bb, h, c: (bb, h, c, 0))
    eg_spec = pl.BlockSpec((1, 1, tc // CHUNK, hd), lambda bb, h, c: (bb, h, c, 0))
    smem = pl.BlockSpec(memory_space=pltpu.SMEM)
    offs = (0, GDN_HEADS, 2 * GDN_HEADS)
    u, w, qg, kd, sd, eg = pl.pallas_call(
        functools.partial(_gdn_prep_kernel, tc=tc, t_valid=t_valid),
        grid=(b, GDN_HEADS, nb),
        in_specs=([smem, smem] + [col(o) for o in offs] + [prev(o) for o in offs]
                  + [state(o) for o in offs] + [wspec(o) for o in offs]
                  + [pl.BlockSpec((1, tc, LANES), lambda bb, h, c: (bb, c, 0))]),
        out_specs=[per_head(hd), per_head(hd), per_head(hd), per_head(hd), per_head(CHUNK), eg_spec],
        out_shape=[jax.ShapeDtypeStruct((b, GDN_HEADS, t, hd), F32)] * 4
                  + [jax.ShapeDtypeStruct((b, GDN_HEADS, t, CHUNK), F32),
                     jax.ShapeDtypeStruct((b, GDN_HEADS, nch, hd), F32)],
        scratch_shapes=[pltpu.VMEM((tc + 8, hd), F32)],
        compiler_params=_cparams(("parallel", "parallel", "parallel")),
        name="gdn_prep",
    )(a_log, dt_bias, qkv, qkv, qkv, qkv, qkv, qkv, conv_prev8, conv_prev8, conv_prev8,
      conv_w, conv_w, conv_w, ab)
    og, s_out = pl.pallas_call(
        functools.partial(_gdn_scan_kernel, tc=tc),
        grid=(b, GDN_HEADS, nb),
        in_specs=[per_head(hd), per_head(hd), per_head(hd), per_head(hd), per_head(CHUNK), eg_spec,
                  pl.BlockSpec((1, tc, hd), lambda bb, h, c: (bb, c, h)),
                  pl.BlockSpec((1, hd), lambda bb, h, c: (0, 0)),
                  pl.BlockSpec((1, 1, hd, hd), lambda bb, h, c: (bb, h, 0, 0))],
        out_specs=[pl.BlockSpec((1, tc, hd), lambda bb, h, c: (bb, c, h)),
                   pl.BlockSpec((1, 1, hd, hd), lambda bb, h, c: (bb, h, 0, 0))],
        out_shape=[jax.ShapeDtypeStruct((b, t, GDN_WIDTH), F32),
                   jax.ShapeDtypeStruct((b, GDN_HEADS, hd, hd), F32)],
        scratch_shapes=[pltpu.VMEM((hd, hd), F32)],
        compiler_params=_cparams(("parallel", "parallel", "arbitrary")),
        name="gdn_scan",
    )(u, w, qg, kd, sd, eg, z, norm_w.reshape(1, hd), s0)
    return og, s_out


def _mlstm_kernel(ib_ref, fb_ref, m0_ref, q_ref, k_ref, v_ref, og_ref, if_ref, c0_ref, n0_ref, nw_ref,
                  h_ref, cout_ref, nout_ref, mout_ref, c_ref, n_ref, m_ref, *, tc, t_valid):
    bb = pl.program_id(0)
    p = pl.program_id(1)
    c = pl.program_id(2)
    vd = ML_V_DIM

    @pl.when(c == 0)
    def _():
        c_ref[...] = c0_ref[0, 0]
        n_ref[...] = n0_ref[0, 0]
        for e in range(2):
            m_ref[e] = jnp.full((1, LANES), m0_ref[bb, 2 * p + e], F32)

    lower, _ = _chunk_masks()
    t1 = lower.astype(BF16)
    t2 = (_iota((CHUNK, CHUNK), 0) > _iota((CHUNK, CHUNK), 1)).astype(F32)
    eye = (_iota((CHUNK, CHUNK), 0) == _iota((CHUNK, CHUNK), 1)).astype(F32)
    ones = jnp.ones((CHUNK, CHUNK), BF16)
    lane_row = _iota((1, LANES), 1)
    lane = _iota((tc, LANES), 1)
    gates = if_ref[0]

    def gate_col(idx):
        col = jnp.sum(jnp.where(lane == idx, gates, 0.0), axis=-1, keepdims=True)
        return jnp.broadcast_to(col, (tc, LANES))

    i_b, lf_b = [], []
    for e in range(2):
        hd = 2 * p + e
        i_b.append(gate_col(hd) + ib_ref[hd])
        lf_b.append(-_softplus(-(gate_col(ML_HEADS + hd) + fb_ref[hd])))

    for n in range(tc // CHUNK):
        sl = slice(n * CHUNK, (n + 1) * CHUNK)
        base = c * tc + n * CHUNK
        valid = lower & ((_iota((CHUNK, CHUNK), 1) + base) < t_valid)
        row_valid = (_iota((CHUNK, LANES), 0) + base) < t_valid
        q2 = q_ref[0, sl, :]
        k2 = k_ref[0, sl, :] * ML_QK_DIM ** -0.5
        cp16 = c_ref[...].astype(BF16)
        n_pair = n_ref[...]
        c_upd = jnp.zeros((LANES, vd), F32)
        n_upd = jnp.zeros((1, LANES), F32)
        a_states = []
        for e in range(2):
            in_head = (lane_row >= ML_QK_DIM * e) & (lane_row < ML_QK_DIM * (e + 1))
            qm = jnp.where(in_head, q2, 0.0)
            km = jnp.where(in_head, k2, 0.0)
            vh = v_ref[0, sl, e * vd:(e + 1) * vd]
            lf = jnp.where(row_valid, lf_b[e][sl], 0.0)
            ic = i_b[e][sl]
            m_prev = m_ref[e]
            bc_b = _mm_exact_lhs(t1, lf)
            dm = (_mm_exact_lhs(t1, lf[:, :CHUNK] * t2)
                  + _mm_exact_lhs(ones, eye * ic[:, :CHUNK]))
            inter = bc_b + m_prev
            row_max = jnp.max(jnp.where(valid, dm, NEG_BIG), axis=-1, keepdims=True)
            m_t = jnp.maximum(inter, row_max)
            w_intra = jnp.where(valid, jnp.exp(jnp.where(valid, dm, 0.0) - m_t[:, :CHUNK]), 0.0)
            w_state = jnp.exp(inter - m_t)
            qm16 = qm.astype(BF16)
            s = _mm_nt(qm16, km.astype(BF16)) * w_intra
            num = w_state * _mm(qm16, cp16) + _mm(s.astype(BF16), vh.astype(BF16))
            qn = jnp.sum(qm * n_pair, axis=-1, keepdims=True)
            den = w_state * qn + jnp.sum(s, axis=-1, keepdims=True)
            hh = num / jnp.maximum(jnp.abs(den), jnp.exp(-m_t))
            m_new = m_t[CHUNK - 1:CHUNK, :]
            bc_last = bc_b[CHUNK - 1:CHUNK, :]
            a_states.append(jnp.exp(bc_last + m_prev - m_new))
            wk = jnp.where(row_valid, jnp.exp(bc_last - bc_b + ic - m_new), 0.0)
            kw = km * wk
            c_upd = c_upd + _mm_tn(kw.astype(BF16), vh.astype(BF16))
            n_upd = n_upd + jnp.sum(kw, axis=0, keepdims=True)
            m_ref[e] = m_new
            og = og_ref[0, sl, e * vd:(e + 1) * vd]
            hn = hh * lax.rsqrt(jnp.mean(hh * hh, axis=-1, keepdims=True) + RMS_EPS) * nw_ref[...]
            h_ref[0, sl, e * vd:(e + 1) * vd] = hn * _sigmoid(og)
        a_rows = jnp.where(_iota((LANES, vd), 0) < ML_QK_DIM, a_states[0], a_states[1])
        c_ref[...] = a_rows * c_ref[...] + c_upd
        n_ref[...] = jnp.where(lane_row < ML_QK_DIM, a_states[0], a_states[1]) * n_pair + n_upd

    @pl.when(c == pl.num_programs(2) - 1)
    def _():
        cout_ref[0, 0] = c_ref[...]
        nout_ref[0, 0] = n_ref[...]
        for e in range(2):
            mout_ref[0, e] = m_ref[e]


def _mlstm(q, k, v, og, ifg, c0, n0, m0, i_bias, f_bias, norm_w, t_valid):
    b, t, _ = q.shape
    tc = min(512, t)
    nb = t // tc
    pairs = ML_HEADS // 2
    smem = pl.BlockSpec(memory_space=pltpu.SMEM)
    qk_spec = pl.BlockSpec((1, tc, LANES), lambda bb, p, c: (bb, c, p))
    v_spec = pl.BlockSpec((1, tc, 2 * ML_V_DIM), lambda bb, p, c: (bb, c, p))
    c_spec = pl.BlockSpec((1, 1, LANES, ML_V_DIM), lambda bb, p, c: (bb, p, 0, 0))
    n_spec = pl.BlockSpec((1, 1, 1, LANES), lambda bb, p, c: (bb, p, 0, 0))
    m_spec = pl.BlockSpec((1, 2, 1, LANES), lambda bb, p, c: (bb, p, 0, 0))
    h, c_out, n_out, m_out = pl.pallas_call(
        functools.partial(_mlstm_kernel, tc=tc, t_valid=t_valid),
        grid=(b, pairs, nb),
        in_specs=[smem, smem, smem, qk_spec, qk_spec, v_spec, v_spec,
                  pl.BlockSpec((1, tc, LANES), lambda bb, p, c: (bb, c, 0)),
                  c_spec, n_spec, pl.BlockSpec((1, ML_V_DIM), lambda bb, p, c: (0, 0))],
        out_specs=[v_spec, c_spec, n_spec, m_spec],
        out_shape=[jax.ShapeDtypeStruct((b, t, ML_V_WIDTH), F32),
                   jax.ShapeDtypeStruct((b, pairs, LANES, ML_V_DIM), F32),
                   jax.ShapeDtypeStruct((b, pairs, 1, LANES), F32),
                   jax.ShapeDtypeStruct((b, ML_HEADS, 1, LANES), F32)],
        scratch_shapes=[pltpu.VMEM((LANES, ML_V_DIM), F32), pltpu.VMEM((1, LANES), F32),
                        pltpu.VMEM((2, 1, LANES), F32)],
        compiler_params=_cparams(("parallel", "parallel", "arbitrary")),
        name="mlstm",
    )(i_bias, f_bias, m0, q, k, v, og, ifg,
      c0.reshape(b, pairs, LANES, ML_V_DIM), n0.reshape(b, pairs, 1, LANES), norm_w.reshape(1, ML_V_DIM))
    return (h, c_out.reshape(b, ML_HEADS, ML_QK_DIM, ML_V_DIM), n_out.reshape(b, ML_HEADS, ML_QK_DIM),
            m_out[:, :, 0, 0])


GROUP_LANE0 = N_EXPERTS


def _router_kernel(x_ref, w_ref, b_ref, g_ref):
    x1, x2, _ = _split3(x_ref[...])
    w1, w2, _ = _split3(w_ref[...])
    logits = _mm(x1, w1) + (_mm(x1, w2) + _mm(x2, w1)) + b_ref[...]
    tm = logits.shape[0]
    lane = _iota((tm, LANES), 1)
    lane_f = lane.astype(F32)
    is_group = (lane >= GROUP_LANE0) & (lane < GROUP_LANE0 + N_GROUPS)
    gl = jnp.where(is_group, logits, NEG_BIG)
    ge = jnp.where(is_group, jnp.exp(gl - jnp.max(gl, axis=-1, keepdims=True)), 0.0)
    gp = ge / jnp.sum(ge, axis=-1, keepdims=True)
    g_gate = jnp.max(gp, axis=-1, keepdims=True)
    g_idx = jnp.min(jnp.where(is_group & (gp == g_gate), lane_f, 1e9), axis=-1, keepdims=True) - GROUP_LANE0
    lo = g_idx * EXPERTS_PER_GROUP
    in_group = (lane_f >= lo) & (lane_f < lo + EXPERTS_PER_GROUP)
    fl = jnp.where(in_group, logits, NEG_BIG)
    fe = jnp.where(in_group, jnp.exp(fl - jnp.max(fl, axis=-1, keepdims=True)), 0.0)
    fp = fe / jnp.sum(fe, axis=-1, keepdims=True)
    w_a = jnp.max(jnp.where(in_group, fp, -1.0), axis=-1, keepdims=True)
    i_a = jnp.min(jnp.where(in_group & (fp == w_a), lane_f, 1e9), axis=-1, keepdims=True)
    rest = in_group & (lane_f != i_a)
    w_b = jnp.max(jnp.where(rest, fp, -1.0), axis=-1, keepdims=True)
    i_b = jnp.min(jnp.where(rest & (fp == w_b), lane_f, 1e9), axis=-1, keepdims=True)
    tot = w_a + w_b
    g_ref[...] = jnp.where(lane_f == i_a, g_gate * (w_a / tot),
                           jnp.where(lane_f == i_b, g_gate * (w_b / tot), 0.0))


def _router(x, w_group, b_group, w_fine, b_fine, tm):
    n, d = x.shape
    tm = min(tm, n)
    w = jnp.concatenate([w_fine.reshape(d, N_EXPERTS), w_group], axis=1)
    w = jnp.pad(w, ((0, 0), (0, LANES - w.shape[1])))
    bias = jnp.concatenate([b_fine.reshape(N_EXPERTS), b_group])
    bias = jnp.pad(bias, (0, LANES - bias.shape[0])).reshape(1, LANES)
    return pl.pallas_call(
        _router_kernel,
        grid=(n // tm,),
        in_specs=[pl.BlockSpec((tm, d), lambda i: (i, 0)),
                  pl.BlockSpec((d, LANES), lambda i: (0, 0)),
                  pl.BlockSpec((1, LANES), lambda i: (0, 0))],
        out_specs=pl.BlockSpec((tm, LANES), lambda i: (i, 0)),
        out_shape=jax.ShapeDtypeStruct((n, LANES), F32),
        compiler_params=_cparams(("parallel",)),
        name="router",
    )(x, w, bias)


def _moe_kernel(x_ref, gt_ref, wg_ref, wu_ref, wd_ref, g_ref, b_ref, o_ref, xb_ref, acc_ref):
    e = pl.program_id(1)

    @pl.when(e == 0)
    def _():
        xb_ref[...] = x_ref[...].astype(BF16)
        acc_ref[...] = jnp.zeros_like(acc_ref)

    gates = gt_ref[...]
    lane = _iota(gates.shape, 1)
    gcol = jnp.sum(jnp.where(lane == e, gates, 0.0), axis=-1, keepdims=True)
    xb = xb_ref[...]
    hg = _mm(xb, wg_ref[0])
    hu = _mm(xb, wu_ref[0])
    hh = (hg * _sigmoid(hg)) * hu * gcol
    acc_ref[...] += _mm(hh.astype(BF16), wd_ref[0])

    @pl.when(e == pl.num_programs(1) - 1)
    def _():
        o_ref[...] = _layer_norm_rows(DN_ALPHA * x_ref[...] + acc_ref[...], g_ref[...], b_ref[...])


def _moe_ln(x, gates, wg, wu, wd, g, b, tm):
    n, d = x.shape
    tm = min(tm, n)
    f = wg.shape[2]
    return pl.pallas_call(
        _moe_kernel,
        grid=(n // tm, N_EXPERTS),
        in_specs=[pl.BlockSpec((tm, d), lambda i, e: (i, 0)),
                  pl.BlockSpec((tm, LANES), lambda i, e: (i, 0)),
                  pl.BlockSpec((1, d, f), lambda i, e: (e, 0, 0)),
                  pl.BlockSpec((1, d, f), lambda i, e: (e, 0, 0)),
                  pl.BlockSpec((1, f, d), lambda i, e: (e, 0, 0)),
                  pl.BlockSpec((1, d), lambda i, e: (0, 0)),
                  pl.BlockSpec((1, d), lambda i, e: (0, 0))],
        out_specs=pl.BlockSpec((tm, d), lambda i, e: (i, 0)),
        out_shape=jax.ShapeDtypeStruct((n, d), F32),
        scratch_shapes=[pltpu.VMEM((tm, d), BF16), pltpu.VMEM((tm, d), F32)],
        compiler_params=_cparams(("parallel", "arbitrary")),
        name="moe_ln",
    )(x, gates, wg, wu, wd, g.reshape(1, d), b.reshape(1, d))


EVEN_WIDTHS = (SB_WIDTH, SB_WIDTH, SB_WIDTH, 3 * GDN_WIDTH, GDN_WIDTH, LANES)
ODD_WIDTHS = (ML_QK_WIDTH, ML_QK_WIDTH, ML_V_WIDTH, ML_V_WIDTH, LANES)
ROW_TILE = 512


def _pad_cols(w, total):
    return jnp.pad(w, ((0, 0), (0, total - w.shape[1])))


def _pad_tokens(a, t_pad):
    return jnp.pad(a, ((0, 0), (0, t_pad - a.shape[1])) + ((0, 0),) * (a.ndim - 2))


def _run_trunk(x, past, conv_buf, gdn_state, ml_state, p):
    b, t, d = x.shape
    n = b * t
    t_pad = max(t, CHUNK)
    xt = x.reshape(n, d)

    q_sb, k_sb, v_sb, qkv_g, z_g, ab = _project(xt, p['even_w_in'], EVEN_WIDTHS, ROW_TILE)
    q3, k3, v3 = (a.reshape(b, t, SB_WIDTH) for a in (q_sb, k_sb, v_sb))
    if past is None:
        o_sb = _sb_prompt(q3, k3, v3, p['sb_bias'])
    else:
        o_sb = _sb_sample(q3, k3, v3, past[0], past[1], past[2], p['sb_bias'])
    qkv3 = qkv_g.reshape(b, t, 3 * GDN_WIDTH)
    conv_prev8 = jnp.pad(conv_buf, ((0, 0), (8 - (GDN_CONV - 1), 0), (0, 0)))
    o_g, s_out = _gdn(_pad_tokens(qkv3, t_pad), _pad_tokens(z_g.reshape(b, t, GDN_WIDTH), t_pad),
                      _pad_tokens(ab.reshape(b, t, LANES), t_pad), conv_prev8, gdn_state,
                      p['gdn_conv_w'], p['gdn_a_log'], p['gdn_dt_bias'], p['gdn_norm_w'], t)
    new_buf = jnp.concatenate([conv_buf, qkv3], axis=1)[:, -(GDN_CONV - 1):]
    merged = jnp.concatenate([o_sb, o_g[:, :t]], axis=-1).reshape(n, SB_WIDTH + GDN_WIDTH)
    xt = _outproj_ln(merged, p['even_w_out'], xt, p['ln_mix_g'][0], p['ln_mix_b'][0], ROW_TILE)
    xt = _ffn(xt, p, 0)

    q_m, k_m, v_m, o_m, ifg = _project(xt, p['odd_w_in'], ODD_WIDTHS, ROW_TILE)
    c0, n0, m0 = ml_state
    h_m, c_out, n_out, m_out = _mlstm(
        _pad_tokens(q_m.reshape(b, t, ML_QK_WIDTH), t_pad), _pad_tokens(k_m.reshape(b, t, ML_QK_WIDTH), t_pad),
        _pad_tokens(v_m.reshape(b, t, ML_V_WIDTH), t_pad), _pad_tokens(o_m.reshape(b, t, ML_V_WIDTH), t_pad),
        _pad_tokens(ifg.reshape(b, t, LANES), t_pad), c0, n0, m0,
        p['mlstm_i_bias'], p['mlstm_f_bias'], p['mlstm_norm_w'], t)
    xt = _outproj_ln(h_m[:, :t].reshape(n, ML_V_WIDTH), p['odd_w_out'], xt,
                     p['ln_mix_g'][1], p['ln_mix_b'][1], ROW_TILE)
    xt = _ffn(xt, p, 1)

    kv_shape = (1, b, t, SB_HEADS, SB_HEAD_DIM)
    return (xt.reshape(b, t, d), k_sb.reshape(kv_shape), v_sb.reshape(kv_shape), new_buf[None], s_out[None],
            c_out[None], n_out[None], m_out[None])


def _ffn(xt, p, layer):
    gates = _router(xt, p['moe_w_group'][layer], p['moe_b_group'][layer], p['moe_w_fine'][layer],
                    p['moe_b_fine'][layer], ROW_TILE)
    return _moe_ln(xt, gates, p['moe_w_gate'][layer], p['moe_w_up'][layer], p['moe_w_down'][layer],
                   p['ln_ffn_g'][layer], p['ln_ffn_b'][layer], ROW_TILE)


def kernel(x_prompt, x_sample, cache_k, cache_v, state_conv, state_gdn, state_mlstm_C, state_mlstm_n,
           state_mlstm_m, page_table, even_w_in, even_w_out, sb_bias, gdn_conv_w, gdn_a_log, gdn_dt_bias,
           gdn_norm_w, odd_w_in, odd_w_out, mlstm_i_bias, mlstm_f_bias, mlstm_norm_w, ln_mix_g, ln_mix_b,
           ln_ffn_g, ln_ffn_b, moe_w_group, moe_b_group, moe_w_fine, moe_b_fine, moe_w_gate, moe_w_up,
           moe_w_down):
    assert DEPTH == 2 and even_w_in.shape[0] == 1 and odd_w_in.shape[0] == 1
    p = {
        'even_w_in': _pad_cols(even_w_in[0], sum(EVEN_WIDTHS)).astype(BF16),
        'even_w_out': even_w_out[0].astype(BF16),
        'sb_bias': sb_bias[0], 'gdn_conv_w': gdn_conv_w[0], 'gdn_a_log': gdn_a_log[0],
        'gdn_dt_bias': gdn_dt_bias[0], 'gdn_norm_w': gdn_norm_w[0],
        'odd_w_in': _pad_cols(odd_w_in[0], sum(ODD_WIDTHS)).astype(BF16),
        'odd_w_out': odd_w_out[0].astype(BF16),
        'mlstm_i_bias': mlstm_i_bias[0], 'mlstm_f_bias': mlstm_f_bias[0], 'mlstm_norm_w': mlstm_norm_w[0],
        'ln_mix_g': ln_mix_g, 'ln_mix_b': ln_mix_b, 'ln_ffn_g': ln_ffn_g, 'ln_ffn_b': ln_ffn_b,
        'moe_w_group': moe_w_group, 'moe_b_group': moe_b_group, 'moe_w_fine': moe_w_fine,
        'moe_b_fine': moe_b_fine, 'moe_w_gate': moe_w_gate.astype(BF16), 'moe_w_up': moe_w_up.astype(BF16),
        'moe_w_down': moe_w_down.astype(BF16),
    }
    bp = x_prompt.shape[0]
    zero_buf = jnp.zeros((bp, GDN_CONV - 1, 3 * GDN_WIDTH), F32)
    zero_s = jnp.zeros((bp, GDN_HEADS, GDN_HEAD_DIM, GDN_HEAD_DIM), F32)
    zero_ml = (jnp.zeros((bp, ML_HEADS, ML_QK_DIM, ML_V_DIM), F32), jnp.zeros((bp, ML_HEADS, ML_QK_DIM), F32),
               jnp.zeros((bp, ML_HEADS), F32))
    (y_p, k_p, v_p, conv_p, gdn_p, mc_p, mn_p, mm_p) = _run_trunk(x_prompt, None, zero_buf, zero_s, zero_ml, p)

    n_pool = cache_k.shape[1]
    pool_k = cache_k.reshape(n_pool, PAGE_SIZE, SB_WIDTH)
    pool_v = cache_v.reshape(n_pool, PAGE_SIZE, SB_WIDTH)
    (y_s, k_s, v_s, conv_s, gdn_s, mc_s, mn_s, mm_s) = _run_trunk(
        x_sample, (pool_k, pool_v, page_table), state_conv[0], state_gdn[0],
        (state_mlstm_C[0], state_mlstm_n[0], state_mlstm_m[0]), p)
    return (y_p, y_s, k_p, v_p, k_s, v_s, conv_p, conv_s, gdn_p, gdn_s, mc_p, mc_s, mn_p, mn_s, mm_p, mm_s)
```

```python
import functools
import math

import jax
import jax.numpy as jnp
from jax import lax
from jax.experimental import pallas as pl
from jax.experimental.pallas import tpu as pltpu

F32 = jnp.float32
BF16 = jnp.bfloat16

D_MODEL = 1024
DEPTH = 2
PAGE_SIZE = 128
SB_HEADS = 8
SB_HEAD_DIM = 64
SB_WIDTH = SB_HEADS * SB_HEAD_DIM
SB_SCALE = SB_HEAD_DIM ** -0.5
GDN_HEADS = 4
GDN_HEAD_DIM = 128
GDN_WIDTH = GDN_HEADS * GDN_HEAD_DIM
GDN_CONV = 4
ML_HEADS = 8
ML_QK_DIM = 64
ML_V_DIM = 128
ML_QK_WIDTH = ML_HEADS * ML_QK_DIM
ML_V_WIDTH = ML_HEADS * ML_V_DIM
N_GROUPS = 4
EXPERTS_PER_GROUP = 4
N_EXPERTS = N_GROUPS * EXPERTS_PER_GROUP
EXPERT_FF = 512
DN_ALPHA = (2 * DEPTH) ** 0.25
LN_EPS = 1e-5
RMS_EPS = 1e-6
L2_EPS = 1e-6

LANES = 128
CHUNK = 64
NEG_BIG = -1e30
VMEM_LIMIT = 56 * 1024 * 1024


def _cparams(sem):
    return pltpu.CompilerParams(dimension_semantics=sem, vmem_limit_bytes=VMEM_LIMIT)


def _mm(a, b):
    return jnp.dot(a, b, preferred_element_type=F32)


def _mm_nt(a, b):
    return lax.dot_general(a, b, (((1,), (1,)), ((), ())), preferred_element_type=F32)


def _mm_tn(a, b):
    return lax.dot_general(a, b, (((0,), (0,)), ((), ())), preferred_element_type=F32)


def _split3(x):
    x1 = x.astype(BF16)
    r1 = x - x1.astype(F32)
    x2 = r1.astype(BF16)
    x3 = (r1 - x2.astype(F32)).astype(BF16)
    return x1, x2, x3


def _mm_exact_lhs(a_bf16, b):
    b1, b2, b3 = _split3(b)
    return _mm(a_bf16, b1) + _mm(a_bf16, b2) + _mm(a_bf16, b3)


def _mm3(a, b):
    a1, a2, _ = _split3(a)
    b1, b2, _ = _split3(b)
    return _mm(a1, b1) + (_mm(a1, b2) + _mm(a2, b1))


def _softplus(x):
    return jnp.maximum(x, 0.0) + jnp.log1p(jnp.exp(-jnp.abs(x)))


def _sigmoid(x):
    return jax.nn.sigmoid(x)


def _iota(shape, dim):
    return lax.broadcasted_iota(jnp.int32, shape, dim)


def _proj_kernel(x_ref, w_ref, *out_refs, widths):
    x = x_ref[...].astype(BF16)
    off = 0
    for o_ref, wd in zip(out_refs, widths):
        o_ref[...] = _mm(x, w_ref[:, off:off + wd])
        off += wd


def _project(x, w_bf16, widths, tm):
    n, d = x.shape
    tm = min(tm, n)
    total = sum(widths)
    return pl.pallas_call(
        functools.partial(_proj_kernel, widths=tuple(widths)),
        grid=(n // tm,),
        in_specs=[pl.BlockSpec((tm, d), lambda i: (i, 0)),
                  pl.BlockSpec((d, total), lambda i: (0, 0))],
        out_specs=[pl.BlockSpec((tm, wd), lambda i: (i, 0)) for wd in widths],
        out_shape=[jax.ShapeDtypeStruct((n, wd), F32) for wd in widths],
        compiler_params=_cparams(("parallel",)),
        name="proj",
    )(x, w_bf16)


def _layer_norm_rows(r, g, b):
    mu = jnp.mean(r, axis=-1, keepdims=True)
    c = r - mu
    var = jnp.mean(c * c, axis=-1, keepdims=True)
    return c * lax.rsqrt(var + LN_EPS) * g + b


def _outproj_ln_kernel(m_ref, w_ref, x_ref, g_ref, b_ref, o_ref):
    y = _mm(m_ref[...].astype(BF16), w_ref[...])
    o_ref[...] = _layer_norm_rows(DN_ALPHA * x_ref[...] + y, g_ref[...], b_ref[...])


def _outproj_ln(merged, w_bf16, x, g, b, tm):
    n, k = merged.shape
    d = x.shape[1]
    tm = min(tm, n)
    return pl.pallas_call(
        _outproj_ln_kernel,
        grid=(n // tm,),
        in_specs=[pl.BlockSpec((tm, k), lambda i: (i, 0)),
                  pl.BlockSpec((k, d), lambda i: (0, 0)),
                  pl.BlockSpec((tm, d), lambda i: (i, 0)),
                  pl.BlockSpec((1, d), lambda i: (0, 0)),
                  pl.BlockSpec((1, d), lambda i: (0, 0))],
        out_specs=pl.BlockSpec((tm, d), lambda i: (i, 0)),
        out_shape=jax.ShapeDtypeStruct((n, d), F32),
        compiler_params=_cparams(("parallel",)),
        name="outproj_ln",
    )(merged, w_bf16, x, g.reshape(1, d), b.reshape(1, d))


SB_SUB = 256
SB_TQ = 512
SB_TKB = 512


def _suffix_sum_matrix(n):
    u = jnp.arange(n)[:, None] > jnp.arange(n)[None, :]
    return jnp.concatenate([u, jnp.ones((n, LANES), bool)], axis=1).astype(BF16)


def _sb_tile(nz, causal, u_ext, r_b):
    tk = nz.shape[1]
    l = jnp.minimum(nz, 0.0) - jnp.log(1.0 + jnp.exp(-jnp.abs(nz)))
    lm = l if causal is None else jnp.where(causal, l, 0.0)
    sums = _mm(lm.astype(BF16), u_ext)
    tail = sums[:, :tk] + jnp.concatenate([r_b] * (tk // LANES), axis=1)
    w = jnp.exp((l - nz) + tail)
    if causal is not None:
        w = jnp.where(causal, w, 0.0)
    return w, r_b + sums[:, tk:]


def _sb_prompt_kernel(qi_ref, kb_ref, bias_ref, u_ref, q_ref, k_ref, v_ref, o_ref, acc_ref, r_ref,
                      *, tq, tkb, sub):
    hp = pl.program_id(1)
    p = pl.program_id(2)
    q0 = qi_ref[p] * tq
    k0 = kb_ref[p] * tkb

    @pl.when(k0 + tkb >= q0 + tq)
    def _():
        acc_ref[...] = jnp.zeros_like(acc_ref)
        r_ref[...] = jnp.zeros_like(r_ref)

    def body(masked):
        lane = _iota((1, LANES), 1)
        nq2 = q_ref[0] * (-SB_SCALE)
        u = u_ref[...]
        upd = jnp.zeros((tq, LANES), F32)
        ones_blk = jnp.where(_iota((sub, LANES), 1) < 3, 1.0, 0.0).astype(BF16)
        for e in range(2):
            in_head = (lane >= SB_HEAD_DIM * e) & (lane < SB_HEAD_DIM * (e + 1))
            nqh = jnp.where(in_head, nq2, 0.0).astype(BF16)
            nb = [part.astype(F32) for part in _split3(jnp.full((1, LANES), -bias_ref[2 * hp + e], F32))]
            bias_row = jnp.where(lane == 0, nb[0], jnp.where(lane == 1, nb[1], jnp.where(lane == 2, nb[2], 0.0)))
            nq_ext = jnp.concatenate([nqh, jnp.broadcast_to(bias_row, (tq, LANES)).astype(BF16)], axis=1)
            r_b = r_ref[e]
            for s in range(tkb // sub - 1, -1, -1):
                ks = jnp.concatenate([k_ref[0, s * sub:(s + 1) * sub, :].astype(BF16), ones_blk], axis=1)
                vs = jnp.where(in_head, v_ref[0, s * sub:(s + 1) * sub, :], 0.0).astype(BF16)
                causal = None
                if masked:
                    causal = (_iota((tq, sub), 1) + (k0 + s * sub)) < (_iota((tq, sub), 0) + q0)
                w, r_b = _sb_tile(_mm_nt(nq_ext, ks), causal, u, r_b)
                upd = upd + _mm(w.astype(BF16), vs)
            r_ref[e] = r_b
        acc_ref[...] += upd

    needs_mask = k0 + tkb > q0

    @pl.when(needs_mask)
    def _():
        body(True)

    @pl.when(jnp.logical_not(needs_mask))
    def _():
        body(False)

    @pl.when(k0 == 0)
    def _():
        o_ref[0] = acc_ref[...]


def _sb_prompt(q, k, v, bias):
    b, t, _ = q.shape
    tq, tkb = min(SB_TQ, t), min(SB_TKB, t)
    sub = min(SB_SUB, tkb)
    qi_list, kb_list = [], []
    for i in range(t // tq):
        for kb in range((i + 1) * tq // tkb - 1, -1, -1):
            qi_list.append(i)
            kb_list.append(kb)
    qi = jnp.asarray(qi_list, jnp.int32)
    kb = jnp.asarray(kb_list, jnp.int32)
    grid_spec = pltpu.PrefetchScalarGridSpec(
        num_scalar_prefetch=2,
        grid=(b, SB_HEADS // 2, len(qi_list)),
        in_specs=[pl.BlockSpec(memory_space=pltpu.SMEM),
                  pl.BlockSpec((sub, sub + LANES), lambda bb, hp, p, qi, kb: (0, 0)),
                  pl.BlockSpec((1, tq, LANES), lambda bb, hp, p, qi, kb: (bb, qi[p], hp)),
                  pl.BlockSpec((1, tkb, LANES), lambda bb, hp, p, qi, kb: (bb, kb[p], hp)),
                  pl.BlockSpec((1, tkb, LANES), lambda bb, hp, p, qi, kb: (bb, kb[p], hp))],
        out_specs=pl.BlockSpec((1, tq, LANES), lambda bb, hp, p, qi, kb: (bb, qi[p], hp)),
        scratch_shapes=[pltpu.VMEM((tq, LANES), F32), pltpu.VMEM((2, tq, LANES), F32)],
    )
    return pl.pallas_call(
        functools.partial(_sb_prompt_kernel, tq=tq, tkb=tkb, sub=sub),
        grid_spec=grid_spec,
        out_shape=jax.ShapeDtypeStruct((b, t, SB_WIDTH), F32),
        compiler_params=_cparams(("parallel", "parallel", "arbitrary")),
        name="sb_prompt",
    )(qi, kb, bias, _suffix_sum_matrix(sub), q, k, v)


PAGES_PER_STEP = 8
PAGE_ROWS = PAGE_SIZE * SB_HEADS


def _sb_sample_kernel(pt_ref, bias_ref, u_ref, q_ref, kn_ref, vn_ref, *rest, t_new):
    k_refs = rest[:PAGES_PER_STEP]
    v_refs = rest[PAGES_PER_STEP:2 * PAGES_PER_STEP]
    o_ref, nq_ref, nb_ref, acc_ref, r_ref = rest[2 * PAGES_PER_STEP:]
    rows = SB_HEADS * t_new
    j = pl.program_id(1)

    def head_rows(ref, h):
        return ref[0, pl.ds(h, PAGE_SIZE, stride=SB_HEADS), :].astype(BF16)

    def process(k_ref, v_ref, masked):
        nz = jnp.concatenate([_mm_nt(nq_ref[h], head_rows(k_ref, h)) for h in range(SB_HEADS)], axis=0)
        causal = None
        if masked:
            causal = _iota((rows, PAGE_SIZE), 1) < (_iota((rows, PAGE_SIZE), 0) % t_new)
        w, r_b = _sb_tile(nz + nb_ref[...], causal, u_ref[...], r_ref[...])
        r_ref[...] = r_b
        for h in range(SB_HEADS):
            acc_ref[h] += _mm(w[h * t_new:(h + 1) * t_new].astype(BF16), head_rows(v_ref, h))

    @pl.when(j == 0)
    def _():
        nq_ref[...] = (q_ref[0] * (-SB_SCALE)).astype(BF16)
        hrow = _iota((rows, LANES), 0) // t_new
        nb = jnp.zeros((rows, LANES), F32)
        for h in range(SB_HEADS):
            nb = jnp.where(hrow == h, -bias_ref[h], nb)
        nb_ref[...] = nb
        acc_ref[...] = jnp.zeros_like(acc_ref)
        r_ref[...] = jnp.zeros_like(r_ref)
        process(kn_ref, vn_ref, True)

    for r in range(PAGES_PER_STEP):
        process(k_refs[r], v_refs[r], False)

    @pl.when(j == pl.num_programs(1) - 1)
    def _():
        o_ref[0] = acc_ref[...]


def _sb_sample(q, k_new, v_new, pool_k, pool_v, page_table, bias):
    b, t_new, _ = q.shape
    n_pages = page_table.shape[1]
    steps = n_pages // PAGES_PER_STEP
    rows = SB_HEADS * t_new

    def as_page(a):
        a = a.reshape(b, t_new * SB_HEADS, SB_HEAD_DIM)
        return jnp.pad(a, ((0, 0), (0, PAGE_ROWS - t_new * SB_HEADS), (0, 0)))

    q_heads = q.reshape(b, t_new, SB_HEADS, SB_HEAD_DIM).transpose(0, 2, 1, 3)

    def page_spec(r):
        def imap(bb, j, pt):
            return (pt[bb, n_pages - 1 - (j * PAGES_PER_STEP + r)], 0, 0)
        return pl.BlockSpec((1, PAGE_ROWS, SB_HEAD_DIM), imap)

    new_spec = pl.BlockSpec((1, PAGE_ROWS, SB_HEAD_DIM), lambda bb, j, pt: (bb, 0, 0))
    qo_spec = pl.BlockSpec((1, SB_HEADS, t_new, SB_HEAD_DIM), lambda bb, j, pt: (bb, 0, 0, 0))
    grid_spec = pltpu.PrefetchScalarGridSpec(
        num_scalar_prefetch=1,
        grid=(b, steps),
        in_specs=([pl.BlockSpec(memory_space=pltpu.SMEM),
                   pl.BlockSpec((PAGE_SIZE, PAGE_SIZE + LANES), lambda bb, j, pt: (0, 0)),
                   qo_spec, new_spec, new_spec]
                  + [page_spec(r) for r in range(PAGES_PER_STEP)]
                  + [page_spec(r) for r in range(PAGES_PER_STEP)]),
        out_specs=qo_spec,
        scratch_shapes=[pltpu.VMEM((SB_HEADS, t_new, SB_HEAD_DIM), BF16), pltpu.VMEM((rows, LANES), F32),
                        pltpu.VMEM((SB_HEADS, t_new, SB_HEAD_DIM), F32), pltpu.VMEM((rows, LANES), F32)],
    )
    o = pl.pallas_call(
        functools.partial(_sb_sample_kernel, t_new=t_new),
        grid_spec=grid_spec,
        out_shape=jax.ShapeDtypeStruct((b, SB_HEADS, t_new, SB_HEAD_DIM), F32),
        compiler_params=_cparams(("parallel", "arbitrary")),
        name="sb_sample",
    )(page_table, bias, _suffix_sum_matrix(PAGE_SIZE), q_heads, as_page(k_new), as_page(v_new),
      *([pool_k] * PAGES_PER_STEP), *([pool_v] * PAGES_PER_STEP))
    return o.transpose(0, 2, 1, 3).reshape(b, t_new, SB_WIDTH)


def _chunk_masks():
    i = _iota((CHUNK, CHUNK), 0)
    j = _iota((CHUNK, CHUNK), 1)
    return i >= j, i > j


def _gdn_prep_kernel(alog_ref, dtb_ref, qr_ref, kr_ref, vr_ref, qp_ref, kp_ref, vp_ref,
                     qs_ref, ks_ref, vs_ref, wq_ref, wk_ref, wv_ref, ab_ref,
                     u_ref, w_ref, qg_ref, kd_ref, sd_ref, eg_ref, xbuf_ref, *, tc, t_valid):
    h = pl.program_id(1)
    c = pl.program_id(2)

    def conv_silu(x_ref, prev_ref, state_ref, wt_ref):
        first = (c == 0).astype(F32)
        xbuf_ref[0:8, :] = first * state_ref[0] + (1.0 - first) * prev_ref[0]
        xbuf_ref[8:, :] = x_ref[0]
        y = jnp.zeros((tc, LANES), F32)
        for i in range(GDN_CONV):
            y = y + xbuf_ref[5 + i:5 + i + tc, :] * wt_ref[i:i + 1, :]
        return y * _sigmoid(y)

    qc = conv_silu(qr_ref, qp_ref, qs_ref, wq_ref)
    kc = conv_silu(kr_ref, kp_ref, ks_ref, wk_ref)
    vc = conv_silu(vr_ref, vp_ref, vs_ref, wv_ref)
    q = qc * lax.rsqrt(jnp.sum(qc * qc, axis=-1, keepdims=True) + L2_EPS) * GDN_HEAD_DIM ** -0.5
    k = kc * lax.rsqrt(jnp.sum(kc * kc, axis=-1, keepdims=True) + L2_EPS)

    ab = ab_ref[0]
    lane = _iota((tc, LANES), 1)
    a_b = jnp.broadcast_to(jnp.sum(jnp.where(lane == h, ab, 0.0), axis=-1, keepdims=True), (tc, LANES))
    b_b = jnp.broadcast_to(jnp.sum(jnp.where(lane == GDN_HEADS + h, ab, 0.0), axis=-1, keepdims=True),
                           (tc, LANES))
    neg_a = -jnp.exp(jnp.full((1, LANES), alog_ref[h], F32))
    g_b = neg_a * _softplus(a_b + dtb_ref[h])
    beta_b = _sigmoid(b_b)
    valid = (_iota((tc, LANES), 0) + c * tc) < t_valid
    g_b = jnp.where(valid, g_b, 0.0)
    beta_b = jnp.where(valid, beta_b, 0.0)

    lower, strict = _chunk_masks()
    t1 = lower.astype(BF16)
    t2 = (_iota((CHUNK, CHUNK), 0) > _iota((CHUNK, CHUNK), 1)).astype(F32)
    eye = (_iota((CHUNK, CHUNK), 0) == _iota((CHUNK, CHUNK), 1)).astype(F32)

    chunks = range(tc // CHUNK)
    sls = [slice(n * CHUNK, (n + 1) * CHUNK) for n in chunks]
    gc_b = [_mm_exact_lhs(t1, g_b[sl]) for sl in sls]
    dg = [_mm_exact_lhs(t1, g_b[sl][:, :CHUNK] * t2) for sl in sls]
    decay = [jnp.where(lower, jnp.exp(jnp.where(lower, d, 0.0)), 0.0) for d in dg]
    kbeta = [k[sl] * beta_b[sl] for sl in sls]
    k16 = [k[sl].astype(BF16) for sl in sls]
    a = [jnp.where(strict, _mm_nt(kbeta[n].astype(BF16), k16[n]) * decay[n], 0.0) for n in chunks]
    tinv = [eye - a[n] for n in chunks]
    pw = a
    for _ in range(int(math.log2(CHUNK)) - 1):
        pw = [_mm3(pw[n], pw[n]) for n in chunks]
        tinv = [tinv[n] + _mm3(tinv[n], pw[n]) for n in chunks]
    sol = [_mm3(tinv[n], jnp.concatenate([vc[sls[n]] * beta_b[sls[n]], kbeta[n] * jnp.exp(gc_b[n])], axis=1))
           for n in chunks]
    for n in chunks:
        sl = sls[n]
        u_ref[0, 0, sl, :] = sol[n][:, :GDN_HEAD_DIM]
        w_ref[0, 0, sl, :] = sol[n][:, GDN_HEAD_DIM:].astype(BF16)
        sd_ref[0, 0, sl, :] = (_mm_nt(q[sl].astype(BF16), k16[n]) * decay[n]).astype(BF16)
        qg_ref[0, 0, sl, :] = (q[sl] * jnp.exp(gc_b[n])).astype(BF16)
        g_last = gc_b[n][CHUNK - 1:CHUNK, :]
        kd_ref[0, 0, sl, :] = (k[sl] * jnp.exp(g_last - gc_b[n])).astype(BF16)
        eg_ref[0, 0, n:n + 1, :] = jnp.exp(g_last)


def _gdn_scan_kernel(u_ref, w_ref, qg_ref, kd_ref, sd_ref, eg_ref, z_ref, nw_ref, s0_ref,
                     o_ref, sout_ref, s_ref, *, tc):
    c = pl.program_id(2)

    @pl.when(c == 0)
    def _():
        s_ref[...] = s0_ref[0, 0]

    s = s_ref[...]
    for n in range(tc // CHUNK):
        sl = slice(n * CHUNK, (n + 1) * CHUNK)
        s16 = s.astype(BF16)
        v_new = u_ref[0, 0, sl, :] - _mm(w_ref[0, 0, sl, :], s16)
        vn16 = v_new.astype(BF16)
        o = _mm(qg_ref[0, 0, sl, :], s16) + _mm(sd_ref[0, 0, sl, :], vn16)
        s = eg_ref[0, 0, n:n + 1, :] * s + _mm_tn(kd_ref[0, 0, sl, :], vn16)
        zg = z_ref[0, sl, :]
        on = o * lax.rsqrt(jnp.mean(o * o, axis=-1, keepdims=True) + RMS_EPS) * nw_ref[...]
        o_ref[0, sl, :] = on * (zg * _sigmoid(zg))
    s_ref[...] = s

    @pl.when(c == pl.num_programs(2) - 1)
    def _():
        sout_ref[0, 0] = s


def _gdn(qkv, z, ab, conv_prev8, s0, conv_w, a_log, dt_bias, norm_w, t_valid):
    b, t, _ = qkv.shape
    tc = min(512, t)
    nb = t // tc
    nch = t // CHUNK
    hd = GDN_HEAD_DIM
    col = lambda off: pl.BlockSpec((1, tc, hd), lambda bb, h, c: (bb, c, h + off))
    prev = lambda off: pl.BlockSpec((1, 8, hd), lambda bb, h, c: (bb, jnp.maximum(c * (tc // 8) - 1, 0), h + off))
    state = lambda off: pl.BlockSpec((1, 8, hd), lambda bb, h, c: (bb, 0, h + off))
    wspec = lambda off: pl.BlockSpec((GDN_CONV, hd), lambda bb, h, c: (0, h + off))
    per_head = lambda width: pl.BlockSpec((1, 1, tc, width), lambda bb, h, c: (bb, h, c, 0))
    eg_spec = pl.BlockSpec((1, 1, tc // CHUNK, hd), lambda bb, h, c: (bb, h, c, 0))
    smem = pl.BlockSpec(memory_space=pltpu.SMEM)
    offs = (0, GDN_HEADS, 2 * GDN_HEADS)
    u, w, qg, kd, sd, eg = pl.pallas_call(
        functools.partial(_gdn_prep_kernel, tc=tc, t_valid=t_valid),
        grid=(b, GDN_HEADS, nb),
        in_specs=([smem, smem] + [col(o) for o in offs] + [prev(o) for o in offs]
                  + [state(o) for o in offs] + [wspec(o) for o in offs]
                  + [pl.BlockSpec((1, tc, LANES), lambda bb, h, c: (bb, c, 0))]),
        out_specs=[per_head(hd), per_head(hd), per_head(hd), per_head(hd), per_head(CHUNK), eg_spec],
        out_shape=[jax.ShapeDtypeStruct((b, GDN_HEADS, t, hd), F32)]
                  + [jax.ShapeDtypeStruct((b, GDN_HEADS, t, hd), BF16)] * 3
                  + [jax.ShapeDtypeStruct((b, GDN_HEADS, t, CHUNK), BF16),
                     jax.ShapeDtypeStruct((b, GDN_HEADS, nch, hd), F32)],
        scratch_shapes=[pltpu.VMEM((tc + 8, hd), F32)],
        compiler_params=_cparams(("parallel", "parallel", "parallel")),
        name="gdn_prep",
    )(a_log, dt_bias, qkv, qkv, qkv, qkv, qkv, qkv, conv_prev8, conv_prev8, conv_prev8,
      conv_w, conv_w, conv_w, ab)
    og, s_out = pl.pallas_call(
        functools.partial(_gdn_scan_kernel, tc=tc),
        grid=(b, GDN_HEADS, nb),
        in_specs=[per_head(hd), per_head(hd), per_head(hd), per_head(hd), per_head(CHUNK), eg_spec,
                  pl.BlockSpec((1, tc, hd), lambda bb, h, c: (bb, c, h)),
                  pl.BlockSpec((1, hd), lambda bb, h, c: (0, 0)),
                  pl.BlockSpec((1, 1, hd, hd), lambda bb, h, c: (bb, h, 0, 0))],
        out_specs=[pl.BlockSpec((1, tc, hd), lambda bb, h, c: (bb, c, h)),
                   pl.BlockSpec((1, 1, hd, hd), lambda bb, h, c: (bb, h, 0, 0))],
        out_shape=[jax.ShapeDtypeStruct((b, t, GDN_WIDTH), F32),
                   jax.ShapeDtypeStruct((b, GDN_HEADS, hd, hd), F32)],
        scratch_shapes=[pltpu.VMEM((hd, hd), F32)],
        compiler_params=_cparams(("parallel", "parallel", "arbitrary")),
        name="gdn_scan",
    )(u, w, qg, kd, sd, eg, z, norm_w.reshape(1, hd), s0)
    return og, s_out


def _mlstm_kernel(ib_ref, fb_ref, m0_ref, q_ref, k_ref, v_ref, og_ref, if_ref, c0_ref, n0_ref, nw_ref,
                  h_ref, cout_ref, nout_ref, mout_ref, c_ref, n_ref, m_ref, *, tc, t_valid):
    bb = pl.program_id(0)
    p = pl.program_id(1)
    c = pl.program_id(2)
    vd = ML_V_DIM

    @pl.when(c == 0)
    def _():
        c_ref[...] = c0_ref[0, 0]
        n_ref[...] = n0_ref[0, 0]
        for e in range(2):
            m_ref[e] = jnp.full((1, LANES), m0_ref[bb, 2 * p + e], F32)

    lower, _ = _chunk_masks()
    t1 = lower.astype(BF16)
    t2 = (_iota((CHUNK, CHUNK), 0) > _iota((CHUNK, CHUNK), 1)).astype(F32)
    eye = (_iota((CHUNK, CHUNK), 0) == _iota((CHUNK, CHUNK), 1)).astype(F32)
    ones = jnp.ones((CHUNK, CHUNK), BF16)
    lane_row = _iota((1, LANES), 1)
    lane = _iota((tc, LANES), 1)
    gates = if_ref[0]

    def gate_col(idx):
        col = jnp.sum(jnp.where(lane == idx, gates, 0.0), axis=-1, keepdims=True)
        return jnp.broadcast_to(col, (tc, LANES))

    i_b, lf_b = [], []
    for e in range(2):
        hd = 2 * p + e
        i_b.append(gate_col(hd) + ib_ref[hd])
        lf_b.append(-_softplus(-(gate_col(ML_HEADS + hd) + fb_ref[hd])))

    chunks = range(tc // CHUNK)
    sls = [slice(n * CHUNK, (n + 1) * CHUNK) for n in chunks]
    valid = [lower & ((_iota((CHUNK, CHUNK), 1) + (c * tc + n * CHUNK)) < t_valid) for n in chunks]
    row_valid = [(_iota((CHUNK, LANES), 0) + (c * tc + n * CHUNK)) < t_valid for n in chunks]
    v16 = [[v_ref[0, sl, e * vd:(e + 1) * vd].astype(BF16) for sl in sls] for e in range(2)]
    qm, km, qm16, qk, bc_b, dm, ic = ([[None] * len(sls) for _ in range(2)] for _ in range(7))
    for e in range(2):
        in_head = (lane_row >= ML_QK_DIM * e) & (lane_row < ML_QK_DIM * (e + 1))
        for n in chunks:
            qm[e][n] = jnp.where(in_head, q_ref[0, sls[n], :], 0.0)
            km[e][n] = jnp.where(in_head, k_ref[0, sls[n], :] * ML_QK_DIM ** -0.5, 0.0)
            qm16[e][n] = qm[e][n].astype(BF16)
            qk[e][n] = _mm_nt(qm16[e][n], km[e][n].astype(BF16))
            lf = jnp.where(row_valid[n], lf_b[e][sls[n]], 0.0)
            ic[e][n] = i_b[e][sls[n]]
            bc_b[e][n] = _mm_exact_lhs(t1, lf)
            dm[e][n] = (_mm_exact_lhs(t1, lf[:, :CHUNK] * t2)
                        + _mm_exact_lhs(ones, eye * ic[e][n][:, :CHUNK]))

    m_t, m_in = ([[None] * len(sls) for _ in range(2)] for _ in range(2))
    for e in range(2):
        m_prev = m_ref[e]
        for n in chunks:
            m_in[e][n] = m_prev
            row_max = jnp.max(jnp.where(valid[n], dm[e][n], NEG_BIG), axis=-1, keepdims=True)
            m_t[e][n] = jnp.maximum(bc_b[e][n] + m_prev, row_max)
            m_prev = m_t[e][n][CHUNK - 1:CHUNK, :]
        m_ref[e] = m_prev

    s, w_state, a_state, c_upd, n_upd = ([[None] * len(sls) for _ in range(2)] for _ in range(5))
    for e in range(2):
        for n in chunks:
            mt = m_t[e][n]
            w_intra = jnp.where(valid[n], jnp.exp(jnp.where(valid[n], dm[e][n], 0.0) - mt[:, :CHUNK]), 0.0)
            w_state[e][n] = jnp.exp(bc_b[e][n] + m_in[e][n] - mt)
            s[e][n] = qk[e][n] * w_intra
            m_new = mt[CHUNK - 1:CHUNK, :]
            bc_last = bc_b[e][n][CHUNK - 1:CHUNK, :]
            a_state[e][n] = jnp.exp(bc_last + m_in[e][n] - m_new)
            wk = jnp.where(row_valid[n], jnp.exp(bc_last - bc_b[e][n] + ic[e][n] - m_new), 0.0)
            kw = km[e][n] * wk
            c_upd[e][n] = _mm_tn(kw.astype(BF16), v16[e][n])
            n_upd[e][n] = jnp.sum(kw, axis=0, keepdims=True)

    c_state, n_state = [c_ref[...]], [n_ref[...]]
    upper_rows = _iota((LANES, vd), 0) < ML_QK_DIM
    for n in chunks:
        a_rows = jnp.where(upper_rows, a_state[0][n], a_state[1][n])
        c_state.append(a_rows * c_state[n] + (c_upd[0][n] + c_upd[1][n]))
        a_lanes = jnp.where(lane_row < ML_QK_DIM, a_state[0][n], a_state[1][n])
        n_state.append(a_lanes * n_state[n] + (n_upd[0][n] + n_upd[1][n]))
    c_ref[...] = c_state[-1]
    n_ref[...] = n_state[-1]

    for n in chunks:
        cp16 = c_state[n].astype(BF16)
        for e in range(2):
            num = w_state[e][n] * _mm(qm16[e][n], cp16) + _mm(s[e][n].astype(BF16), v16[e][n])
            qn = jnp.sum(qm[e][n] * n_state[n], axis=-1, keepdims=True)
            den = w_state[e][n] * qn + jnp.sum(s[e][n], axis=-1, keepdims=True)
            hh = num / jnp.maximum(jnp.abs(den), jnp.exp(-m_t[e][n]))
            og = og_ref[0, sls[n], e * vd:(e + 1) * vd]
            hn = hh * lax.rsqrt(jnp.mean(hh * hh, axis=-1, keepdims=True) + RMS_EPS) * nw_ref[...]
            h_ref[0, sls[n], e * vd:(e + 1) * vd] = hn * _sigmoid(og)

    @pl.when(c == pl.num_programs(2) - 1)
    def _():
        cout_ref[0, 0] = c_ref[...]
        nout_ref[0, 0] = n_ref[...]
        for e in range(2):
            mout_ref[0, e] = m_ref[e]


def _mlstm(q, k, v, og, ifg, c0, n0, m0, i_bias, f_bias, norm_w, t_valid):
    b, t, _ = q.shape
    tc = min(512, t)
    nb = t // tc
    pairs = ML_HEADS // 2
    smem = pl.BlockSpec(memory_space=pltpu.SMEM)
    qk_spec = pl.BlockSpec((1, tc, LANES), lambda bb, p, c: (bb, c, p))
    v_spec = pl.BlockSpec((1, tc, 2 * ML_V_DIM), lambda bb, p, c: (bb, c, p))
    c_spec = pl.BlockSpec((1, 1, LANES, ML_V_DIM), lambda bb, p, c: (bb, p, 0, 0))
    n_spec = pl.BlockSpec((1, 1, 1, LANES), lambda bb, p, c: (bb, p, 0, 0))
    m_spec = pl.BlockSpec((1, 2, 1, LANES), lambda bb, p, c: (bb, p, 0, 0))
    h, c_out, n_out, m_out = pl.pallas_call(
        functools.partial(_mlstm_kernel, tc=tc, t_valid=t_valid),
        grid=(b, pairs, nb),
        in_specs=[smem, smem, smem, qk_spec, qk_spec, v_spec, v_spec,
                  pl.BlockSpec((1, tc, LANES), lambda bb, p, c: (bb, c, 0)),
                  c_spec, n_spec, pl.BlockSpec((1, ML_V_DIM), lambda bb, p, c: (0, 0))],
        out_specs=[v_spec, c_spec, n_spec, m_spec],
        out_shape=[jax.ShapeDtypeStruct((b, t, ML_V_WIDTH), F32),
                   jax.ShapeDtypeStruct((b, pairs, LANES, ML_V_DIM), F32),
                   jax.ShapeDtypeStruct((b, pairs, 1, LANES), F32),
                   jax.ShapeDtypeStruct((b, ML_HEADS, 1, LANES), F32)],
        scratch_shapes=[pltpu.VMEM((LANES, ML_V_DIM), F32), pltpu.VMEM((1, LANES), F32),
                        pltpu.VMEM((2, 1, LANES), F32)],
        compiler_params=_cparams(("parallel", "parallel", "arbitrary")),
        name="mlstm",
    )(i_bias, f_bias, m0, q, k, v, og, ifg,
      c0.reshape(b, pairs, LANES, ML_V_DIM), n0.reshape(b, pairs, 1, LANES), norm_w.reshape(1, ML_V_DIM))
    return (h, c_out.reshape(b, ML_HEADS, ML_QK_DIM, ML_V_DIM), n_out.reshape(b, ML_HEADS, ML_QK_DIM),
            m_out[:, :, 0, 0])


GROUP_LANE0 = N_EXPERTS


def _router_kernel(x_ref, w_ref, b_ref, g_ref):
    x1, x2, _ = _split3(x_ref[...])
    w1, w2, _ = _split3(w_ref[...])
    logits = _mm(x1, w1) + (_mm(x1, w2) + _mm(x2, w1)) + b_ref[...]
    tm = logits.shape[0]
    lane = _iota((tm, LANES), 1)
    lane_f = lane.astype(F32)
    is_group = (lane >= GROUP_LANE0) & (lane < GROUP_LANE0 + N_GROUPS)
    gl = jnp.where(is_group, logits, NEG_BIG)
    ge = jnp.where(is_group, jnp.exp(gl - jnp.max(gl, axis=-1, keepdims=True)), 0.0)
    gp = ge / jnp.sum(ge, axis=-1, keepdims=True)
    g_gate = jnp.max(gp, axis=-1, keepdims=True)
    g_idx = jnp.min(jnp.where(is_group & (gp == g_gate), lane_f, 1e9), axis=-1, keepdims=True) - GROUP_LANE0
    lo = g_idx * EXPERTS_PER_GROUP
    in_group = (lane_f >= lo) & (lane_f < lo + EXPERTS_PER_GROUP)
    fl = jnp.where(in_group, logits, NEG_BIG)
    fe = jnp.where(in_group, jnp.exp(fl - jnp.max(fl, axis=-1, keepdims=True)), 0.0)
    fp = fe / jnp.sum(fe, axis=-1, keepdims=True)
    w_a = jnp.max(jnp.where(in_group, fp, -1.0), axis=-1, keepdims=True)
    i_a = jnp.min(jnp.where(in_group & (fp == w_a), lane_f, 1e9), axis=-1, keepdims=True)
    rest = in_group & (lane_f != i_a)
    w_b = jnp.max(jnp.where(rest, fp, -1.0), axis=-1, keepdims=True)
    i_b = jnp.min(jnp.where(rest & (fp == w_b), lane_f, 1e9), axis=-1, keepdims=True)
    tot = w_a + w_b
    g_ref[...] = jnp.where(lane_f == i_a, g_gate * (w_a / tot),
                           jnp.where(lane_f == i_b, g_gate * (w_b / tot), 0.0))


def _router(x, w_group, b_group, w_fine, b_fine, tm):
    n, d = x.shape
    tm = min(tm, n)
    w = jnp.concatenate([w_fine.reshape(d, N_EXPERTS), w_group], axis=1)
    w = jnp.pad(w, ((0, 0), (0, LANES - w.shape[1])))
    bias = jnp.concatenate([b_fine.reshape(N_EXPERTS), b_group])
    bias = jnp.pad(bias, (0, LANES - bias.shape[0])).reshape(1, LANES)
    return pl.pallas_call(
        _router_kernel,
        grid=(n // tm,),
        in_specs=[pl.BlockSpec((tm, d), lambda i: (i, 0)),
                  pl.BlockSpec((d, LANES), lambda i: (0, 0)),
                  pl.BlockSpec((1, LANES), lambda i: (0, 0))],
        out_specs=pl.BlockSpec((tm, LANES), lambda i: (i, 0)),
        out_shape=jax.ShapeDtypeStruct((n, LANES), F32),
        compiler_params=_cparams(("parallel",)),
        name="router",
    )(x, w, bias)


def _moe_kernel(x_ref, gt_ref, wg_ref, wu_ref, wd_ref, g_ref, b_ref, o_ref, xb_ref, acc_ref):
    e = pl.program_id(1)

    @pl.when(e == 0)
    def _():
        xb_ref[...] = x_ref[...].astype(BF16)
        acc_ref[...] = jnp.zeros_like(acc_ref)

    gates = gt_ref[...]
    lane = _iota(gates.shape, 1)
    gcol = jnp.sum(jnp.where(lane == e, gates, 0.0), axis=-1, keepdims=True)
    xb = xb_ref[...]
    hg = _mm(xb, wg_ref[0])
    hu = _mm(xb, wu_ref[0])
    hh = (hg * _sigmoid(hg)) * hu * gcol
    acc_ref[...] += _mm(hh.astype(BF16), wd_ref[0])

    @pl.when(e == pl.num_programs(1) - 1)
    def _():
        o_ref[...] = _layer_norm_rows(DN_ALPHA * x_ref[...] + acc_ref[...], g_ref[...], b_ref[...])


def _moe_ln(x, gates, wg, wu, wd, g, b, tm):
    n, d = x.shape
    tm = min(tm, n)
    f = wg.shape[2]
    return pl.pallas_call(
        _moe_kernel,
        grid=(n // tm, N_EXPERTS),
        in_specs=[pl.BlockSpec((tm, d), lambda i, e: (i, 0)),
                  pl.BlockSpec((tm, LANES), lambda i, e: (i, 0)),
                  pl.BlockSpec((1, d, f), lambda i, e: (e, 0, 0)),
                  pl.BlockSpec((1, d, f), lambda i, e: (e, 0, 0)),
                  pl.BlockSpec((1, f, d), lambda i, e: (e, 0, 0)),
                  pl.BlockSpec((1, d), lambda i, e: (0, 0)),
                  pl.BlockSpec((1, d), lambda i, e: (0, 0))],
        out_specs=pl.BlockSpec((tm, d), lambda i, e: (i, 0)),
        out_shape=jax.ShapeDtypeStruct((n, d), F32),
        scratch_shapes=[pltpu.VMEM((tm, d), BF16), pltpu.VMEM((tm, d), F32)],
        compiler_params=_cparams(("parallel", "arbitrary")),
        name="moe_ln",
    )(x, gates, wg, wu, wd, g.reshape(1, d), b.reshape(1, d))


EVEN_WIDTHS = (SB_WIDTH, SB_WIDTH, SB_WIDTH, 3 * GDN_WIDTH, GDN_WIDTH, LANES)
ODD_WIDTHS = (ML_QK_WIDTH, ML_QK_WIDTH, ML_V_WIDTH, ML_V_WIDTH, LANES)
ROW_TILE = 512


def _pad_cols(w, total):
    return jnp.pad(w, ((0, 0), (0, total - w.shape[1])))


def _pad_tokens(a, t_pad):
    return jnp.pad(a, ((0, 0), (0, t_pad - a.shape[1])) + ((0, 0),) * (a.ndim - 2))


def _run_trunk(x, past, conv_buf, gdn_state, ml_state, p):
    b, t, d = x.shape
    n = b * t
    t_pad = max(t, CHUNK)
    xt = x.reshape(n, d)

    q_sb, k_sb, v_sb, qkv_g, z_g, ab = _project(xt, p['even_w_in'], EVEN_WIDTHS, ROW_TILE)
    q3, k3, v3 = (a.reshape(b, t, SB_WIDTH) for a in (q_sb, k_sb, v_sb))
    if past is None:
        o_sb = _sb_prompt(q3, k3, v3, p['sb_bias'])
    else:
        o_sb = _sb_sample(q3, k3, v3, past[0], past[1], past[2], p['sb_bias'])
    qkv3 = qkv_g.reshape(b, t, 3 * GDN_WIDTH)
    conv_prev8 = jnp.pad(conv_buf, ((0, 0), (8 - (GDN_CONV - 1), 0), (0, 0)))
    o_g, s_out = _gdn(_pad_tokens(qkv3, t_pad), _pad_tokens(z_g.reshape(b, t, GDN_WIDTH), t_pad),
                      _pad_tokens(ab.reshape(b, t, LANES), t_pad), conv_prev8, gdn_state,
                      p['gdn_conv_w'], p['gdn_a_log'], p['gdn_dt_bias'], p['gdn_norm_w'], t)
    new_buf = jnp.concatenate([conv_buf, qkv3], axis=1)[:, -(GDN_CONV - 1):]
    merged = jnp.concatenate([o_sb, o_g[:, :t]], axis=-1).reshape(n, SB_WIDTH + GDN_WIDTH)
    xt = _outproj_ln(merged, p['even_w_out'], xt, p['ln_mix_g'][0], p['ln_mix_b'][0], ROW_TILE)
    xt = _ffn(xt, p, 0)

    q_m, k_m, v_m, o_m, ifg = _project(xt, p['odd_w_in'], ODD_WIDTHS, ROW_TILE)
    c0, n0, m0 = ml_state
    h_m, c_out, n_out, m_out = _mlstm(
        _pad_tokens(q_m.reshape(b, t, ML_QK_WIDTH), t_pad), _pad_tokens(k_m.reshape(b, t, ML_QK_WIDTH), t_pad),
        _pad_tokens(v_m.reshape(b, t, ML_V_WIDTH), t_pad), _pad_tokens(o_m.reshape(b, t, ML_V_WIDTH), t_pad),
        _pad_tokens(ifg.reshape(b, t, LANES), t_pad), c0, n0, m0,
        p['mlstm_i_bias'], p['mlstm_f_bias'], p['mlstm_norm_w'], t)
    xt = _outproj_ln(h_m[:, :t].reshape(n, ML_V_WIDTH), p['odd_w_out'], xt,
                     p['ln_mix_g'][1], p['ln_mix_b'][1], ROW_TILE)
    xt = _ffn(xt, p, 1)

    kv_shape = (1, b, t, SB_HEADS, SB_HEAD_DIM)
    return (xt.reshape(b, t, d), k_sb.reshape(kv_shape), v_sb.reshape(kv_shape), new_buf[None], s_out[None],
            c_out[None], n_out[None], m_out[None])


def _ffn(xt, p, layer):
    gates = _router(xt, p['moe_w_group'][layer], p['moe_b_group'][layer], p['moe_w_fine'][layer],
                    p['moe_b_fine'][layer], ROW_TILE)
    return _moe_ln(xt, gates, p['moe_w_gate'][layer], p['moe_w_up'][layer], p['moe_w_down'][layer],
                   p['ln_ffn_g'][layer], p['ln_ffn_b'][layer], ROW_TILE)


def kernel(x_prompt, x_sample, cache_k, cache_v, state_conv, state_gdn, state_mlstm_C, state_mlstm_n,
           state_mlstm_m, page_table, even_w_in, even_w_out, sb_bias, gdn_conv_w, gdn_a_log, gdn_dt_bias,
           gdn_norm_w, odd_w_in, odd_w_out, mlstm_i_bias, mlstm_f_bias, mlstm_norm_w, ln_mix_g, ln_mix_b,
           ln_ffn_g, ln_ffn_b, moe_w_group, moe_b_group, moe_w_fine, moe_b_fine, moe_w_gate, moe_w_up,
           moe_w_down):
    assert DEPTH == 2 and even_w_in.shape[0] == 1 and odd_w_in.shape[0] == 1
    p = {
        'even_w_in': _pad_cols(even_w_in[0], sum(EVEN_WIDTHS)).astype(BF16),
        'even_w_out': even_w_out[0].astype(BF16),
        'sb_bias': sb_bias[0], 'gdn_conv_w': gdn_conv_w[0], 'gdn_a_log': gdn_a_log[0],
        'gdn_dt_bias': gdn_dt_bias[0], 'gdn_norm_w': gdn_norm_w[0],
        'odd_w_in': _pad_cols(odd_w_in[0], sum(ODD_WIDTHS)).astype(BF16),
        'odd_w_out': odd_w_out[0].astype(BF16),
        'mlstm_i_bias': mlstm_i_bias[0], 'mlstm_f_bias': mlstm_f_bias[0], 'mlstm_norm_w': mlstm_norm_w[0],
        'ln_mix_g': ln_mix_g, 'ln_mix_b': ln_mix_b, 'ln_ffn_g': ln_ffn_g, 'ln_ffn_b': ln_ffn_b,
        'moe_w_group': moe_w_group, 'moe_b_group': moe_b_group, 'moe_w_fine': moe_w_fine,
        'moe_b_fine': moe_b_fine, 'moe_w_gate': moe_w_gate.astype(BF16), 'moe_w_up': moe_w_up.astype(BF16),
        'moe_w_down': moe_w_down.astype(BF16),
    }
    bp = x_prompt.shape[0]
    zero_buf = jnp.zeros((bp, GDN_CONV - 1, 3 * GDN_WIDTH), F32)
    zero_s = jnp.zeros((bp, GDN_HEADS, GDN_HEAD_DIM, GDN_HEAD_DIM), F32)
    zero_ml = (jnp.zeros((bp, ML_HEADS, ML_QK_DIM, ML_V_DIM), F32), jnp.zeros((bp, ML_HEADS, ML_QK_DIM), F32),
               jnp.zeros((bp, ML_HEADS), F32))
    (y_p, k_p, v_p, conv_p, gdn_p, mc_p, mn_p, mm_p) = _run_trunk(x_prompt, None, zero_buf, zero_s, zero_ml, p)

    n_pool = cache_k.shape[1]
    pool_k = cache_k.reshape(n_pool, PAGE_ROWS, SB_HEAD_DIM)
    pool_v = cache_v.reshape(n_pool, PAGE_ROWS, SB_HEAD_DIM)
    (y_s, k_s, v_s, conv_s, gdn_s, mc_s, mn_s, mm_s) = _run_trunk(
        x_sample, (pool_k, pool_v, page_table), state_conv[0], state_gdn[0],
        (state_mlstm_C[0], state_mlstm_n[0], state_mlstm_m[0]), p)
    return (y_p, y_s, k_p, v_p, k_s, v_s, conv_p, conv_s, gdn_p, gdn_s, mc_p, mc_s, mn_p, mn_s, mm_p, mm_s)
```

```python
import functools
import math

import jax
import jax.numpy as jnp
from jax import lax
from jax.experimental import pallas as pl
from jax.experimental.pallas import tpu as pltpu

F32 = jnp.float32
BF16 = jnp.bfloat16

D_MODEL = 1024
DEPTH = 2
PAGE_SIZE = 128
SB_HEADS = 8
SB_HEAD_DIM = 64
SB_WIDTH = SB_HEADS * SB_HEAD_DIM
SB_SCALE = SB_HEAD_DIM ** -0.5
GDN_HEADS = 4
GDN_HEAD_DIM = 128
GDN_WIDTH = GDN_HEADS * GDN_HEAD_DIM
GDN_CONV = 4
ML_HEADS = 8
ML_QK_DIM = 64
ML_V_DIM = 128
ML_QK_WIDTH = ML_HEADS * ML_QK_DIM
ML_V_WIDTH = ML_HEADS * ML_V_DIM
N_GROUPS = 4
EXPERTS_PER_GROUP = 4
N_EXPERTS = N_GROUPS * EXPERTS_PER_GROUP
EXPERT_FF = 512
DN_ALPHA = (2 * DEPTH) ** 0.25
LN_EPS = 1e-5
RMS_EPS = 1e-6
L2_EPS = 1e-6

LANES = 128
CHUNK = 64
NEG_BIG = -1e30
VMEM_LIMIT = 56 * 1024 * 1024


def _cparams(sem):
    return pltpu.CompilerParams(dimension_semantics=sem, vmem_limit_bytes=VMEM_LIMIT)


def _mm(a, b):
    return jnp.dot(a, b, preferred_element_type=F32)


def _mm_nt(a, b):
    return lax.dot_general(a, b, (((1,), (1,)), ((), ())), preferred_element_type=F32)


def _mm_tn(a, b):
    return lax.dot_general(a, b, (((0,), (0,)), ((), ())), preferred_element_type=F32)


def _split3(x):
    x1 = x.astype(BF16)
    r1 = x - x1.astype(F32)
    x2 = r1.astype(BF16)
    x3 = (r1 - x2.astype(F32)).astype(BF16)
    return x1, x2, x3


def _mm_exact_lhs(a_bf16, b):
    b1, b2, b3 = _split3(b)
    return _mm(a_bf16, b1) + _mm(a_bf16, b2) + _mm(a_bf16, b3)


def _mm3(a, b):
    a1, a2, _ = _split3(a)
    b1, b2, _ = _split3(b)
    return _mm(a1, b1) + (_mm(a1, b2) + _mm(a2, b1))


def _softplus(x):
    return jnp.maximum(x, 0.0) + jnp.log1p(jnp.exp(-jnp.abs(x)))


def _sigmoid(x):
    return jax.nn.sigmoid(x)


def _iota(shape, dim):
    return lax.broadcasted_iota(jnp.int32, shape, dim)


def _proj_kernel(x_ref, w_ref, *out_refs, widths):
    x = x_ref[...].astype(BF16)
    off = 0
    for o_ref, wd in zip(out_refs, widths):
        o_ref[...] = _mm(x, w_ref[:, off:off + wd])
        off += wd


def _project(x, w_bf16, widths, tm):
    n, d = x.shape
    tm = min(tm, n)
    total = sum(widths)
    return pl.pallas_call(
        functools.partial(_proj_kernel, widths=tuple(widths)),
        grid=(n // tm,),
        in_specs=[pl.BlockSpec((tm, d), lambda i: (i, 0)),
                  pl.BlockSpec((d, total), lambda i: (0, 0))],
        out_specs=[pl.BlockSpec((tm, wd), lambda i: (i, 0)) for wd in widths],
        out_shape=[jax.ShapeDtypeStruct((n, wd), F32) for wd in widths],
        compiler_params=_cparams(("parallel",)),
        name="proj",
    )(x, w_bf16)


def _layer_norm_rows(r, g, b):
    mu = jnp.mean(r, axis=-1, keepdims=True)
    c = r - mu
    var = jnp.mean(c * c, axis=-1, keepdims=True)
    return c * lax.rsqrt(var + LN_EPS) * g + b


def _outproj_ln_kernel(m_ref, w_ref, x_ref, g_ref, b_ref, o_ref):
    y = _mm(m_ref[...].astype(BF16), w_ref[...])
    o_ref[...] = _layer_norm_rows(DN_ALPHA * x_ref[...] + y, g_ref[...], b_ref[...])


def _outproj_ln(merged, w_bf16, x, g, b, tm):
    n, k = merged.shape
    d = x.shape[1]
    tm = min(tm, n)
    return pl.pallas_call(
        _outproj_ln_kernel,
        grid=(n // tm,),
        in_specs=[pl.BlockSpec((tm, k), lambda i: (i, 0)),
                  pl.BlockSpec((k, d), lambda i: (0, 0)),
                  pl.BlockSpec((tm, d), lambda i: (i, 0)),
                  pl.BlockSpec((1, d), lambda i: (0, 0)),
                  pl.BlockSpec((1, d), lambda i: (0, 0))],
        out_specs=pl.BlockSpec((tm, d), lambda i: (i, 0)),
        out_shape=jax.ShapeDtypeStruct((n, d), F32),
        compiler_params=_cparams(("parallel",)),
        name="outproj_ln",
    )(merged, w_bf16, x, g.reshape(1, d), b.reshape(1, d))


SB_SUB = 256
SB_TQ = 512
SB_TKB = 512


LOG2E = math.log2(math.e)


def _suffix_sum_matrix(n):
    return (jnp.arange(n)[:, None] > jnp.arange(n)[None, :]).astype(BF16)


def _sb_chains(chains, u):
    logs = [[jnp.minimum(nz, 0.0) - jnp.log(1.0 + jnp.exp2(jnp.abs(nz) * -LOG2E)) for nz in tiles]
            for tiles, _, _ in chains]
    masked = [[l if c is None else jnp.where(c, l, 0.0) for l, c in zip(ls, causals)]
              for ls, (_, causals, _) in zip(logs, chains)]
    sums = [[_mm(lm.astype(BF16), u) for lm in lms] for lms in masked]
    out = []
    for ls, lms, ss, (tiles, causals, r_b) in zip(logs, masked, sums, chains):
        ws = []
        for nz, l, lm, s, c in zip(tiles, ls, lms, ss, causals):
            tk = nz.shape[1]
            tail = s + jnp.concatenate([r_b] * (tk // LANES), axis=1)
            w = jnp.exp2(((l - nz) + tail) * LOG2E)
            ws.append(w if c is None else jnp.where(c, w, 0.0))
            r_b = r_b + jnp.broadcast_to(s[:, :1] + lm[:, :1], r_b.shape)
        out.append((ws, r_b))
    return out


def _sb_prompt_kernel(qi_ref, kb_ref, bias_ref, u_ref, q_ref, k_ref, v_ref, o_ref, acc_ref, r_ref,
                      *, tq, tkb, sub):
    hp = pl.program_id(1)
    p = pl.program_id(2)
    q0 = qi_ref[p] * tq
    k0 = kb_ref[p] * tkb

    @pl.when(k0 + tkb >= q0 + tq)
    def _():
        acc_ref[...] = jnp.zeros_like(acc_ref)
        r_ref[...] = jnp.zeros_like(r_ref)

    def body(masked):
        lane = _iota((1, LANES), 1)
        nq2 = q_ref[0] * (-SB_SCALE)
        u = u_ref[...]
        ones_blk = jnp.where(_iota((sub, LANES), 1) < 3, 1.0, 0.0).astype(BF16)
        subs = range(tkb // sub - 1, -1, -1)
        k_ext = [jnp.concatenate([k_ref[0, s * sub:(s + 1) * sub, :].astype(BF16), ones_blk], axis=1) for s in subs]
        causals = [None] * len(subs)
        if masked:
            causals = [(_iota((tq, sub), 1) + (k0 + s * sub)) < (_iota((tq, sub), 0) + q0) for s in subs]
        chains, in_heads = [], []
        for e in range(2):
            in_head = (lane >= SB_HEAD_DIM * e) & (lane < SB_HEAD_DIM * (e + 1))
            nqh = jnp.where(in_head, nq2, 0.0).astype(BF16)
            nb = [part.astype(F32) for part in _split3(jnp.full((1, LANES), -bias_ref[2 * hp + e], F32))]
            bias_row = jnp.where(lane == 0, nb[0], jnp.where(lane == 1, nb[1], jnp.where(lane == 2, nb[2], 0.0)))
            nq_ext = jnp.concatenate([nqh, jnp.broadcast_to(bias_row, (tq, LANES)).astype(BF16)], axis=1)
            chains.append(([_mm_nt(nq_ext, ks) for ks in k_ext], causals, r_ref[e]))
            in_heads.append(in_head)
        upd = jnp.zeros((tq, LANES), F32)
        for e, (ws, r_b) in enumerate(_sb_chains(chains, u)):
            r_ref[e] = r_b
            for w, s in zip(ws, subs):
                vs = jnp.where(in_heads[e], v_ref[0, s * sub:(s + 1) * sub, :], 0.0).astype(BF16)
                upd = upd + _mm(w.astype(BF16), vs)
        acc_ref[...] += upd

    needs_mask = k0 + tkb > q0

    @pl.when(needs_mask)
    def _():
        body(True)

    @pl.when(jnp.logical_not(needs_mask))
    def _():
        body(False)

    @pl.when(k0 == 0)
    def _():
        o_ref[0] = acc_ref[...]


def _sb_prompt(q, k, v, bias):
    b, t, _ = q.shape
    tq, tkb = min(SB_TQ, t), min(SB_TKB, t)
    sub = min(SB_SUB, tkb)
    qi_list, kb_list = [], []
    for i in range(t // tq):
        for kb in range((i + 1) * tq // tkb - 1, -1, -1):
            qi_list.append(i)
            kb_list.append(kb)
    qi = jnp.asarray(qi_list, jnp.int32)
    kb = jnp.asarray(kb_list, jnp.int32)
    grid_spec = pltpu.PrefetchScalarGridSpec(
        num_scalar_prefetch=2,
        grid=(b, SB_HEADS // 2, len(qi_list)),
        in_specs=[pl.BlockSpec(memory_space=pltpu.SMEM),
                  pl.BlockSpec((sub, sub), lambda bb, hp, p, qi, kb: (0, 0)),
                  pl.BlockSpec((1, tq, LANES), lambda bb, hp, p, qi, kb: (bb, qi[p], hp)),
                  pl.BlockSpec((1, tkb, LANES), lambda bb, hp, p, qi, kb: (bb, kb[p], hp)),
                  pl.BlockSpec((1, tkb, LANES), lambda bb, hp, p, qi, kb: (bb, kb[p], hp))],
        out_specs=pl.BlockSpec((1, tq, LANES), lambda bb, hp, p, qi, kb: (bb, qi[p], hp)),
        scratch_shapes=[pltpu.VMEM((tq, LANES), F32), pltpu.VMEM((2, tq, LANES), F32)],
    )
    return pl.pallas_call(
        functools.partial(_sb_prompt_kernel, tq=tq, tkb=tkb, sub=sub),
        grid_spec=grid_spec,
        out_shape=jax.ShapeDtypeStruct((b, t, SB_WIDTH), F32),
        compiler_params=_cparams(("parallel", "parallel", "arbitrary")),
        name="sb_prompt",
    )(qi, kb, bias, _suffix_sum_matrix(sub), q, k, v)


PAGES_PER_STEP = 8


def _sb_sample_kernel(pt_ref, bias_ref, u_ref, q_ref, kn_ref, vn_ref, *rest, t_new):
    k_refs = rest[:PAGES_PER_STEP]
    v_refs = rest[PAGES_PER_STEP:2 * PAGES_PER_STEP]
    o_ref, nq_ref, nb_ref, acc_ref, r_ref = rest[2 * PAGES_PER_STEP:]
    rows = SB_HEADS * t_new
    j = pl.program_id(1)

    def process(page_k_refs, page_v_refs, masked):
        n_pg = len(page_k_refs)
        keys = lambda refs, h: jnp.concatenate([ref[0, h].astype(BF16) for ref in refs], axis=1)
        nz = jnp.concatenate([_mm(nq_ref[h], keys(page_k_refs, h)) for h in range(SB_HEADS)], axis=0)
        nz = nz + jnp.concatenate([nb_ref[...]] * n_pg, axis=1)
        causal = None
        if masked:
            causal = _iota((rows, PAGE_SIZE), 1) < (_iota((rows, PAGE_SIZE), 0) % t_new)
        tiles = [nz[:, g * PAGE_SIZE:(g + 1) * PAGE_SIZE] for g in range(n_pg)]
        (ws, r_b), = _sb_chains([(tiles, [causal] * n_pg, r_ref[...])], u_ref[...])
        r_ref[...] = r_b
        w = jnp.concatenate(ws, axis=1)
        for h in range(SB_HEADS):
            acc_ref[h] += _mm_nt(w[h * t_new:(h + 1) * t_new].astype(BF16), keys(page_v_refs, h))

    @pl.when(j == 0)
    def _():
        nq_ref[...] = (q_ref[0] * (-SB_SCALE)).astype(BF16)
        hrow = _iota((rows, LANES), 0) // t_new
        nb = jnp.zeros((rows, LANES), F32)
        for h in range(SB_HEADS):
            nb = jnp.where(hrow == h, -bias_ref[h], nb)
        nb_ref[...] = nb
        acc_ref[...] = jnp.zeros_like(acc_ref)
        r_ref[...] = jnp.zeros_like(r_ref)
        process([kn_ref], [vn_ref], True)

    process(k_refs, v_refs, False)

    @pl.when(j == pl.num_programs(1) - 1)
    def _():
        o_ref[0] = acc_ref[...]


def _sb_sample(q, k_new, v_new, pool_k, pool_v, page_table, bias):
    b, t_new, _ = q.shape
    n_pages = page_table.shape[1]
    steps = n_pages // PAGES_PER_STEP
    rows = SB_HEADS * t_new
    page_block = (1, SB_HEADS, SB_HEAD_DIM, PAGE_SIZE)

    def as_page(a):
        a = a.reshape(b, t_new, SB_HEADS, SB_HEAD_DIM).transpose(0, 2, 3, 1)
        return jnp.pad(a, ((0, 0), (0, 0), (0, 0), (0, PAGE_SIZE - t_new)))

    q_heads = q.reshape(b, t_new, SB_HEADS, SB_HEAD_DIM).transpose(0, 2, 1, 3)

    def page_spec(r):
        def imap(bb, j, pt):
            return (pt[bb, n_pages - 1 - (j * PAGES_PER_STEP + r)], 0, 0, 0)
        return pl.BlockSpec(page_block, imap)

    new_spec = pl.BlockSpec(page_block, lambda bb, j, pt: (bb, 0, 0, 0))
    qo_spec = pl.BlockSpec((1, SB_HEADS, t_new, SB_HEAD_DIM), lambda bb, j, pt: (bb, 0, 0, 0))
    grid_spec = pltpu.PrefetchScalarGridSpec(
        num_scalar_prefetch=1,
        grid=(b, steps),
        in_specs=([pl.BlockSpec(memory_space=pltpu.SMEM),
                   pl.BlockSpec((PAGE_SIZE, PAGE_SIZE), lambda bb, j, pt: (0, 0)),
                   qo_spec, new_spec, new_spec]
                  + [page_spec(r) for r in range(PAGES_PER_STEP)]
                  + [page_spec(r) for r in range(PAGES_PER_STEP)]),
        out_specs=qo_spec,
        scratch_shapes=[pltpu.VMEM((SB_HEADS, t_new, SB_HEAD_DIM), BF16), pltpu.VMEM((rows, LANES), F32),
                        pltpu.VMEM((SB_HEADS, t_new, SB_HEAD_DIM), F32), pltpu.VMEM((rows, LANES), F32)],
    )
    o = pl.pallas_call(
        functools.partial(_sb_sample_kernel, t_new=t_new),
        grid_spec=grid_spec,
        out_shape=jax.ShapeDtypeStruct((b, SB_HEADS, t_new, SB_HEAD_DIM), F32),
        compiler_params=_cparams(("parallel", "arbitrary")),
        name="sb_sample",
    )(page_table, bias, _suffix_sum_matrix(PAGE_SIZE), q_heads, as_page(k_new), as_page(v_new),
      *([pool_k] * PAGES_PER_STEP), *([pool_v] * PAGES_PER_STEP))
    return o.transpose(0, 2, 1, 3).reshape(b, t_new, SB_WIDTH)


def _chunk_masks():
    i = _iota((CHUNK, CHUNK), 0)
    j = _iota((CHUNK, CHUNK), 1)
    return i >= j, i > j


def _gdn_prep_kernel(alog_ref, dtb_ref, qr_ref, kr_ref, vr_ref, qp_ref, kp_ref, vp_ref,
                     qs_ref, ks_ref, vs_ref, wq_ref, wk_ref, wv_ref, ab_ref,
                     u_ref, w_ref, qg_ref, kd_ref, sd_ref, eg_ref, xbuf_ref, *, tc, t_valid):
    h = pl.program_id(1)
    c = pl.program_id(2)

    def conv_silu(x_ref, prev_ref, state_ref, wt_ref):
        first = (c == 0).astype(F32)
        xbuf_ref[0:8, :] = first * state_ref[0] + (1.0 - first) * prev_ref[0]
        xbuf_ref[8:, :] = x_ref[0]
        y = jnp.zeros((tc, LANES), F32)
        for i in range(GDN_CONV):
            y = y + xbuf_ref[5 + i:5 + i + tc, :] * wt_ref[i:i + 1, :]
        return y * _sigmoid(y)

    qc = conv_silu(qr_ref, qp_ref, qs_ref, wq_ref)
    kc = conv_silu(kr_ref, kp_ref, ks_ref, wk_ref)
    vc = conv_silu(vr_ref, vp_ref, vs_ref, wv_ref)
    q = qc * lax.rsqrt(jnp.sum(qc * qc, axis=-1, keepdims=True) + L2_EPS) * GDN_HEAD_DIM ** -0.5
    k = kc * lax.rsqrt(jnp.sum(kc * kc, axis=-1, keepdims=True) + L2_EPS)

    ab = ab_ref[0]
    lane = _iota((tc, LANES), 1)
    a_b = jnp.broadcast_to(jnp.sum(jnp.where(lane == h, ab, 0.0), axis=-1, keepdims=True), (tc, LANES))
    b_b = jnp.broadcast_to(jnp.sum(jnp.where(lane == GDN_HEADS + h, ab, 0.0), axis=-1, keepdims=True),
                           (tc, LANES))
    neg_a = -jnp.exp(jnp.full((1, LANES), alog_ref[h], F32))
    g_b = neg_a * _softplus(a_b + dtb_ref[h])
    beta_b = _sigmoid(b_b)
    valid = (_iota((tc, LANES), 0) + c * tc) < t_valid
    g_b = jnp.where(valid, g_b, 0.0)
    beta_b = jnp.where(valid, beta_b, 0.0)

    lower, strict = _chunk_masks()
    t1 = lower.astype(BF16)
    t2 = (_iota((CHUNK, CHUNK), 0) > _iota((CHUNK, CHUNK), 1)).astype(F32)
    eye = (_iota((CHUNK, CHUNK), 0) == _iota((CHUNK, CHUNK), 1)).astype(F32)

    chunks = range(tc // CHUNK)
    sls = [slice(n * CHUNK, (n + 1) * CHUNK) for n in chunks]
    gc_b = [_mm_exact_lhs(t1, g_b[sl]) for sl in sls]
    dg = [_mm_exact_lhs(t1, g_b[sl][:, :CHUNK] * t2) for sl in sls]
    decay = [jnp.where(lower, jnp.exp(jnp.where(lower, d, 0.0)), 0.0) for d in dg]
    kbeta = [k[sl] * beta_b[sl] for sl in sls]
    k16 = [k[sl].astype(BF16) for sl in sls]
    a = [jnp.where(strict, _mm_nt(kbeta[n].astype(BF16), k16[n]) * decay[n], 0.0) for n in chunks]
    tinv = [eye - a[n] for n in chunks]
    pw = a
    for _ in range(int(math.log2(CHUNK)) - 1):
        pw = [_mm3(pw[n], pw[n]) for n in chunks]
        tinv = [tinv[n] + _mm3(tinv[n], pw[n]) for n in chunks]
    sol = [_mm3(tinv[n], jnp.concatenate([vc[sls[n]] * beta_b[sls[n]], kbeta[n] * jnp.exp(gc_b[n])], axis=1))
           for n in chunks]
    for n in chunks:
        sl = sls[n]
        u_ref[0, 0, sl, :] = sol[n][:, :GDN_HEAD_DIM]
        w_ref[0, 0, sl, :] = sol[n][:, GDN_HEAD_DIM:].astype(BF16)
        sd_ref[0, 0, sl, :] = (_mm_nt(q[sl].astype(BF16), k16[n]) * decay[n]).astype(BF16)
        qg_ref[0, 0, sl, :] = (q[sl] * jnp.exp(gc_b[n])).astype(BF16)
        g_last = gc_b[n][CHUNK - 1:CHUNK, :]
        kd_ref[0, 0, sl, :] = (k[sl] * jnp.exp(g_last - gc_b[n])).astype(BF16)
        eg_ref[0, 0, n:n + 1, :] = jnp.exp(g_last)


def _gdn_scan_kernel(u_ref, w_ref, qg_ref, kd_ref, sd_ref, eg_ref, z_ref, nw_ref, s0_ref,
                     o_ref, sout_ref, s_ref, *, tc):
    c = pl.program_id(2)

    @pl.when(c == 0)
    def _():
        s_ref[...] = s0_ref[0, 0]

    s = s_ref[...]
    for n in range(tc // CHUNK):
        sl = slice(n * CHUNK, (n + 1) * CHUNK)
        s16 = s.astype(BF16)
        v_new = u_ref[0, 0, sl, :] - _mm(w_ref[0, 0, sl, :], s16)
        vn16 = v_new.astype(BF16)
        o = _mm(qg_ref[0, 0, sl, :], s16) + _mm(sd_ref[0, 0, sl, :], vn16)
        s = eg_ref[0, 0, n:n + 1, :] * s + _mm_tn(kd_ref[0, 0, sl, :], vn16)
        zg = z_ref[0, sl, :]
        on = o * lax.rsqrt(jnp.mean(o * o, axis=-1, keepdims=True) + RMS_EPS) * nw_ref[...]
        o_ref[0, sl, :] = on * (zg * _sigmoid(zg))
    s_ref[...] = s

    @pl.when(c == pl.num_programs(2) - 1)
    def _():
        sout_ref[0, 0] = s


def _gdn(qkv, z, ab, conv_prev8, s0, conv_w, a_log, dt_bias, norm_w, t_valid):
    b, t, _ = qkv.shape
    tc = min(512, t)
    nb = t // tc
    nch = t // CHUNK
    hd = GDN_HEAD_DIM
    col = lambda off: pl.BlockSpec((1, tc, hd), lambda bb, h, c: (bb, c, h + off))
    prev = lambda off: pl.BlockSpec((1, 8, hd), lambda bb, h, c: (bb, jnp.maximum(c * (tc // 8) - 1, 0), h + off))
    state = lambda off: pl.BlockSpec((1, 8, hd), lambda bb, h, c: (bb, 0, h + off))
    wspec = lambda off: pl.BlockSpec((GDN_CONV, hd), lambda bb, h, c: (0, h + off))
    per_head = lambda width: pl.BlockSpec((1, 1, tc, width), lambda bb, h, c: (bb, h, c, 0))
    eg_spec = pl.BlockSpec((1, 1, tc // CHUNK, hd), lambda bb, h, c: (bb, h, c, 0))
    smem = pl.BlockSpec(memory_space=pltpu.SMEM)
    offs = (0, GDN_HEADS, 2 * GDN_HEADS)
    u, w, qg, kd, sd, eg = pl.pallas_call(
        functools.partial(_gdn_prep_kernel, tc=tc, t_valid=t_valid),
        grid=(b, GDN_HEADS, nb),
        in_specs=([smem, smem] + [col(o) for o in offs] + [prev(o) for o in offs]
                  + [state(o) for o in offs] + [wspec(o) for o in offs]
                  + [pl.BlockSpec((1, tc, LANES), lambda bb, h, c: (bb, c, 0))]),
        out_specs=[per_head(hd), per_head(hd), per_head(hd), per_head(hd), per_head(CHUNK), eg_spec],
        out_shape=[jax.ShapeDtypeStruct((b, GDN_HEADS, t, hd), F32)]
                  + [jax.ShapeDtypeStruct((b, GDN_HEADS, t, hd), BF16)] * 3
                  + [jax.ShapeDtypeStruct((b, GDN_HEADS, t, CHUNK), BF16),
                     jax.ShapeDtypeStruct((b, GDN_HEADS, nch, hd), F32)],
        scratch_shapes=[pltpu.VMEM((tc + 8, hd), F32)],
        compiler_params=_cparams(("parallel", "parallel", "parallel")),
        name="gdn_prep",
    )(a_log, dt_bias, qkv, qkv, qkv, qkv, qkv, qkv, conv_prev8, conv_prev8, conv_prev8,
      conv_w, conv_w, conv_w, ab)
    og, s_out = pl.pallas_call(
        functools.partial(_gdn_scan_kernel, tc=tc),
        grid=(b, GDN_HEADS, nb),
        in_specs=[per_head(hd), per_head(hd), per_head(hd), per_head(hd), per_head(CHUNK), eg_spec,
                  pl.BlockSpec((1, tc, hd), lambda bb, h, c: (bb, c, h)),
                  pl.BlockSpec((1, hd), lambda bb, h, c: (0, 0)),
                  pl.BlockSpec((1, 1, hd, hd), lambda bb, h, c: (bb, h, 0, 0))],
        out_specs=[pl.BlockSpec((1, tc, hd), lambda bb, h, c: (bb, c, h)),
                   pl.BlockSpec((1, 1, hd, hd), lambda bb, h, c: (bb, h, 0, 0))],
        out_shape=[jax.ShapeDtypeStruct((b, t, GDN_WIDTH), F32),
                   jax.ShapeDtypeStruct((b, GDN_HEADS, hd, hd), F32)],
        scratch_shapes=[pltpu.VMEM((hd, hd), F32)],
        compiler_params=_cparams(("parallel", "parallel", "arbitrary")),
        name="gdn_scan",
    )(u, w, qg, kd, sd, eg, z, norm_w.reshape(1, hd), s0)
    return og, s_out


def _mlstm_kernel(ib_ref, fb_ref, m0_ref, q_ref, k_ref, v_ref, og_ref, if_ref, c0_ref, n0_ref, nw_ref,
                  h_ref, cout_ref, nout_ref, mout_ref, c_ref, n_ref, m_ref, *, tc, t_valid):
    bb = pl.program_id(0)
    p = pl.program_id(1)
    c = pl.program_id(2)
    vd = ML_V_DIM

    @pl.when(c == 0)
    def _():
        c_ref[...] = c0_ref[0, 0]
        n_ref[...] = n0_ref[0, 0]
        for e in range(2):
            m_ref[e] = jnp.full((1, LANES), m0_ref[bb, 2 * p + e], F32)

    lower, _ = _chunk_masks()
    t1 = lower.astype(BF16)
    t2 = (_iota((CHUNK, CHUNK), 0) > _iota((CHUNK, CHUNK), 1)).astype(F32)
    eye = (_iota((CHUNK, CHUNK), 0) == _iota((CHUNK, CHUNK), 1)).astype(F32)
    ones = jnp.ones((CHUNK, CHUNK), BF16)
    lane_row = _iota((1, LANES), 1)
    lane = _iota((tc, LANES), 1)
    gates = if_ref[0]

    def gate_col(idx):
        col = jnp.sum(jnp.where(lane == idx, gates, 0.0), axis=-1, keepdims=True)
        return jnp.broadcast_to(col, (tc, LANES))

    i_b, lf_b = [], []
    for e in range(2):
        hd = 2 * p + e
        i_b.append(gate_col(hd) + ib_ref[hd])
        lf_b.append(-_softplus(-(gate_col(ML_HEADS + hd) + fb_ref[hd])))

    chunks = range(tc // CHUNK)
    sls = [slice(n * CHUNK, (n + 1) * CHUNK) for n in chunks]
    valid = [lower & ((_iota((CHUNK, CHUNK), 1) + (c * tc + n * CHUNK)) < t_valid) for n in chunks]
    row_valid = [(_iota((CHUNK, LANES), 0) + (c * tc + n * CHUNK)) < t_valid for n in chunks]
    v16 = [[v_ref[0, sl, e * vd:(e + 1) * vd].astype(BF16) for sl in sls] for e in range(2)]
    qm, km, qm16, qk, bc_b, dm, ic = ([[None] * len(sls) for _ in range(2)] for _ in range(7))
    for e in range(2):
        in_head = (lane_row >= ML_QK_DIM * e) & (lane_row < ML_QK_DIM * (e + 1))
        for n in chunks:
            qm[e][n] = jnp.where(in_head, q_ref[0, sls[n], :], 0.0)
            km[e][n] = jnp.where(in_head, k_ref[0, sls[n], :] * ML_QK_DIM ** -0.5, 0.0)
            qm16[e][n] = qm[e][n].astype(BF16)
            qk[e][n] = _mm_nt(qm16[e][n], km[e][n].astype(BF16))
            lf = jnp.where(row_valid[n], lf_b[e][sls[n]], 0.0)
            ic[e][n] = i_b[e][sls[n]]
            bc_b[e][n] = _mm_exact_lhs(t1, lf)
            dm[e][n] = (_mm_exact_lhs(t1, lf[:, :CHUNK] * t2)
                        + _mm_exact_lhs(ones, eye * ic[e][n][:, :CHUNK]))

    m_t, m_in = ([[None] * len(sls) for _ in range(2)] for _ in range(2))
    for e in range(2):
        m_prev = m_ref[e]
        for n in chunks:
            m_in[e][n] = m_prev
            row_max = jnp.max(jnp.where(valid[n], dm[e][n], NEG_BIG), axis=-1, keepdims=True)
            m_t[e][n] = jnp.maximum(bc_b[e][n] + m_prev, row_max)
            m_prev = m_t[e][n][CHUNK - 1:CHUNK, :]
        m_ref[e] = m_prev

    s, w_state, a_state, c_upd, n_upd = ([[None] * len(sls) for _ in range(2)] for _ in range(5))
    for e in range(2):
        for n in chunks:
            mt = m_t[e][n]
            w_intra = jnp.where(valid[n], jnp.exp(jnp.where(valid[n], dm[e][n], 0.0) - mt[:, :CHUNK]), 0.0)
            w_state[e][n] = jnp.exp(bc_b[e][n] + m_in[e][n] - mt)
            s[e][n] = qk[e][n] * w_intra
            m_new = mt[CHUNK - 1:CHUNK, :]
            bc_last = bc_b[e][n][CHUNK - 1:CHUNK, :]
            a_state[e][n] = jnp.exp(bc_last + m_in[e][n] - m_new)
            wk = jnp.where(row_valid[n], jnp.exp(bc_last - bc_b[e][n] + ic[e][n] - m_new), 0.0)
            kw = km[e][n] * wk
            c_upd[e][n] = _mm_tn(kw.astype(BF16), v16[e][n])
            n_upd[e][n] = jnp.sum(kw, axis=0, keepdims=True)

    c_state, n_state = [c_ref[...]], [n_ref[...]]
    upper_rows = _iota((LANES, vd), 0) < ML_QK_DIM
    for n in chunks:
        a_rows = jnp.where(upper_rows, a_state[0][n], a_state[1][n])
        c_state.append(a_rows * c_state[n] + (c_upd[0][n] + c_upd[1][n]))
        a_lanes = jnp.where(lane_row < ML_QK_DIM, a_state[0][n], a_state[1][n])
        n_state.append(a_lanes * n_state[n] + (n_upd[0][n] + n_upd[1][n]))
    c_ref[...] = c_state[-1]
    n_ref[...] = n_state[-1]

    for n in chunks:
        cp16 = c_state[n].astype(BF16)
        for e in range(2):
            num = w_state[e][n] * _mm(qm16[e][n], cp16) + _mm(s[e][n].astype(BF16), v16[e][n])
            qn = jnp.sum(qm[e][n] * n_state[n], axis=-1, keepdims=True)
            den = w_state[e][n] * qn + jnp.sum(s[e][n], axis=-1, keepdims=True)
            hh = num / jnp.maximum(jnp.abs(den), jnp.exp(-m_t[e][n]))
            og = og_ref[0, sls[n], e * vd:(e + 1) * vd]
            hn = hh * lax.rsqrt(jnp.mean(hh * hh, axis=-1, keepdims=True) + RMS_EPS) * nw_ref[...]
            h_ref[0, sls[n], e * vd:(e + 1) * vd] = hn * _sigmoid(og)

    @pl.when(c == pl.num_programs(2) - 1)
    def _():
        cout_ref[0, 0] = c_ref[...]
        nout_ref[0, 0] = n_ref[...]
        for e in range(2):
            mout_ref[0, e] = m_ref[e]


def _mlstm(q, k, v, og, ifg, c0, n0, m0, i_bias, f_bias, norm_w, t_valid):
    b, t, _ = q.shape
    tc = min(512, t)
    nb = t // tc
    pairs = ML_HEADS // 2
    smem = pl.BlockSpec(memory_space=pltpu.SMEM)
    qk_spec = pl.BlockSpec((1, tc, LANES), lambda bb, p, c: (bb, c, p))
    v_spec = pl.BlockSpec((1, tc, 2 * ML_V_DIM), lambda bb, p, c: (bb, c, p))
    c_spec = pl.BlockSpec((1, 1, LANES, ML_V_DIM), lambda bb, p, c: (bb, p, 0, 0))
    n_spec = pl.BlockSpec((1, 1, 1, LANES), lambda bb, p, c: (bb, p, 0, 0))
    m_spec = pl.BlockSpec((1, 2, 1, LANES), lambda bb, p, c: (bb, p, 0, 0))
    h, c_out, n_out, m_out = pl.pallas_call(
        functools.partial(_mlstm_kernel, tc=tc, t_valid=t_valid),
        grid=(b, pairs, nb),
        in_specs=[smem, smem, smem, qk_spec, qk_spec, v_spec, v_spec,
                  pl.BlockSpec((1, tc, LANES), lambda bb, p, c: (bb, c, 0)),
                  c_spec, n_spec, pl.BlockSpec((1, ML_V_DIM), lambda bb, p, c: (0, 0))],
        out_specs=[v_spec, c_spec, n_spec, m_spec],
        out_shape=[jax.ShapeDtypeStruct((b, t, ML_V_WIDTH), F32),
                   jax.ShapeDtypeStruct((b, pairs, LANES, ML_V_DIM), F32),
                   jax.ShapeDtypeStruct((b, pairs, 1, LANES), F32),
                   jax.ShapeDtypeStruct((b, ML_HEADS, 1, LANES), F32)],
        scratch_shapes=[pltpu.VMEM((LANES, ML_V_DIM), F32), pltpu.VMEM((1, LANES), F32),
                        pltpu.VMEM((2, 1, LANES), F32)],
        compiler_params=_cparams(("parallel", "parallel", "arbitrary")),
        name="mlstm",
    )(i_bias, f_bias, m0, q, k, v, og, ifg,
      c0.reshape(b, pairs, LANES, ML_V_DIM), n0.reshape(b, pairs, 1, LANES), norm_w.reshape(1, ML_V_DIM))
    return (h, c_out.reshape(b, ML_HEADS, ML_QK_DIM, ML_V_DIM), n_out.reshape(b, ML_HEADS, ML_QK_DIM),
            m_out[:, :, 0, 0])


GROUP_LANE0 = N_EXPERTS
ROUTE_LANE0 = N_EXPERTS
PAIRS_PER_GROUP = EXPERTS_PER_GROUP * (EXPERTS_PER_GROUP - 1) // 2
N_CLASSES = N_GROUPS * PAIRS_PER_GROUP
MOE_TM = 256
SPARSE_MIN_TOKENS = 4096


def _router_kernel(x_ref, w_ref, b_ref, g_ref, *xr_ref):
    x1, x2, _ = _split3(x_ref[...])
    w1, w2, _ = _split3(w_ref[...])
    logits = _mm(x1, w1) + (_mm(x1, w2) + _mm(x2, w1)) + b_ref[...]
    tm = logits.shape[0]
    lane = _iota((tm, LANES), 1)
    lane_f = lane.astype(F32)
    is_group = (lane >= GROUP_LANE0) & (lane < GROUP_LANE0 + N_GROUPS)
    gl = jnp.where(is_group, logits, NEG_BIG)
    ge = jnp.where(is_group, jnp.exp(gl - jnp.max(gl, axis=-1, keepdims=True)), 0.0)
    gp = ge / jnp.sum(ge, axis=-1, keepdims=True)
    g_gate = jnp.max(gp, axis=-1, keepdims=True)
    g_idx = jnp.min(jnp.where(is_group & (gp == g_gate), lane_f, 1e9), axis=-1, keepdims=True) - GROUP_LANE0
    lo = g_idx * EXPERTS_PER_GROUP
    in_group = (lane_f >= lo) & (lane_f < lo + EXPERTS_PER_GROUP)
    fl = jnp.where(in_group, logits, NEG_BIG)
    fe = jnp.where(in_group, jnp.exp(fl - jnp.max(fl, axis=-1, keepdims=True)), 0.0)
    fp = fe / jnp.sum(fe, axis=-1, keepdims=True)
    w_a = jnp.max(jnp.where(in_group, fp, -1.0), axis=-1, keepdims=True)
    i_a = jnp.min(jnp.where(in_group & (fp == w_a), lane_f, 1e9), axis=-1, keepdims=True)
    rest = in_group & (lane_f != i_a)
    w_b = jnp.max(jnp.where(rest, fp, -1.0), axis=-1, keepdims=True)
    i_b = jnp.min(jnp.where(rest & (fp == w_b), lane_f, 1e9), axis=-1, keepdims=True)
    tot = w_a + w_b
    gate_a = g_gate * (w_a / tot)
    gate_b = g_gate * (w_b / tot)
    e_lo = jnp.minimum(i_a, i_b)
    e_hi = jnp.maximum(i_a, i_b)
    a_loc = e_lo - lo
    pair = a_loc * (7.0 - a_loc) * 0.5 + (e_hi - e_lo - 1.0)
    record = (e_lo, e_hi, jnp.where(i_a < i_b, gate_a, gate_b), jnp.where(i_a < i_b, gate_b, gate_a),
              g_idx * PAIRS_PER_GROUP + pair)
    out = jnp.where(lane_f == i_a, gate_a, jnp.where(lane_f == i_b, gate_b, 0.0))
    for off, val in enumerate(record):
        out = jnp.where(lane == ROUTE_LANE0 + off, val, out)
    g_ref[...] = out
    if xr_ref:
        d = x_ref.shape[1]
        xr_ref[0][:, :d] = x_ref[...]
        xr_ref[0][:, d:] = out


def _router(x, w_group, b_group, w_fine, b_fine, tm, with_rows):
    n, d = x.shape
    tm = min(tm, n)
    w = jnp.concatenate([w_fine.reshape(d, N_EXPERTS), w_group], axis=1)
    w = jnp.pad(w, ((0, 0), (0, LANES - w.shape[1])))
    bias = jnp.concatenate([b_fine.reshape(N_EXPERTS), b_group])
    bias = jnp.pad(bias, (0, LANES - bias.shape[0])).reshape(1, LANES)
    out_specs = [pl.BlockSpec((tm, LANES), lambda i: (i, 0))]
    out_shape = [jax.ShapeDtypeStruct((n, LANES), F32)]
    if with_rows:
        out_specs.append(pl.BlockSpec((tm, d + LANES), lambda i: (i, 0)))
        out_shape.append(jax.ShapeDtypeStruct((n, d + LANES), F32))
    return pl.pallas_call(
        _router_kernel,
        grid=(n // tm,),
        in_specs=[pl.BlockSpec((tm, d), lambda i: (i, 0)),
                  pl.BlockSpec((d, LANES), lambda i: (0, 0)),
                  pl.BlockSpec((1, LANES), lambda i: (0, 0))],
        out_specs=out_specs,
        out_shape=out_shape,
        compiler_params=_cparams(("parallel",)),
        name="router",
    )(x, w, bias)


def _moe_kernel(x_ref, gt_ref, wg_ref, wu_ref, wd_ref, g_ref, b_ref, o_ref, xb_ref, acc_ref):
    e = pl.program_id(1)

    @pl.when(e == 0)
    def _():
        xb_ref[...] = x_ref[...].astype(BF16)
        acc_ref[...] = jnp.zeros_like(acc_ref)

    gates = gt_ref[...]
    lane = _iota(gates.shape, 1)
    gcol = jnp.sum(jnp.where(lane == e, gates, 0.0), axis=-1, keepdims=True)
    xb = xb_ref[...]
    hg = _mm(xb, wg_ref[0])
    hu = _mm(xb, wu_ref[0])
    hh = (hg * _sigmoid(hg)) * hu * gcol
    acc_ref[...] += _mm(hh.astype(BF16), wd_ref[0])

    @pl.when(e == pl.num_programs(1) - 1)
    def _():
        o_ref[...] = _layer_norm_rows(DN_ALPHA * x_ref[...] + acc_ref[...], g_ref[...], b_ref[...])


def _moe_ln(x, gates, wg, wu, wd, g, b, tm):
    n, d = x.shape
    tm = min(tm, n)
    f = wg.shape[2]
    return pl.pallas_call(
        _moe_kernel,
        grid=(n // tm, N_EXPERTS),
        in_specs=[pl.BlockSpec((tm, d), lambda i, e: (i, 0)),
                  pl.BlockSpec((tm, LANES), lambda i, e: (i, 0)),
                  pl.BlockSpec((1, d, f), lambda i, e: (e, 0, 0)),
                  pl.BlockSpec((1, d, f), lambda i, e: (e, 0, 0)),
                  pl.BlockSpec((1, f, d), lambda i, e: (e, 0, 0)),
                  pl.BlockSpec((1, d), lambda i, e: (0, 0)),
                  pl.BlockSpec((1, d), lambda i, e: (0, 0))],
        out_specs=pl.BlockSpec((tm, d), lambda i, e: (i, 0)),
        out_shape=jax.ShapeDtypeStruct((n, d), F32),
        scratch_shapes=[pltpu.VMEM((tm, d), BF16), pltpu.VMEM((tm, d), F32)],
        compiler_params=_cparams(("parallel", "arbitrary")),
        name="moe_ln",
    )(x, gates, wg, wu, wd, g.reshape(1, d), b.reshape(1, d))


def _route_plan(cls, n, tm):
    n_rows = n + N_CLASSES * tm
    pairs = [(a, b) for a in range(EXPERTS_PER_GROUP) for b in range(a + 1, EXPERTS_PER_GROUP)]
    lo_tab = jnp.asarray([EXPERTS_PER_GROUP * g + a for g in range(N_GROUPS) for a, _ in pairs], jnp.int32)
    hi_tab = jnp.asarray([EXPERTS_PER_GROUP * g + b for g in range(N_GROUPS) for _, b in pairs], jnp.int32)
    order = jnp.argsort(cls, stable=True).astype(jnp.int32)
    counts = jnp.sum((cls[:, None] == jnp.arange(N_CLASSES)[None, :]).astype(jnp.int32), axis=0)
    padded = (counts + tm - 1) // tm * tm
    c_start = jnp.cumsum(counts) - counts
    p_end = jnp.cumsum(padded)
    p_start = p_end - padded
    rows = jnp.arange(n_rows, dtype=jnp.int32)
    row_cls = jnp.minimum(jnp.searchsorted(p_end, rows, side='right'), N_CLASSES - 1).astype(jnp.int32)
    off = rows - p_start[row_cls]
    real = off < counts[row_cls]
    token = order[jnp.clip(c_start[row_cls] + off, 0, n - 1)]
    src = jnp.where(real, token, 0).astype(jnp.int32)
    dst = jnp.where(real, token, n + rows % tm).astype(jnp.int32)
    tile_cls = row_cls[::tm]
    n_used = (p_end[-1] // tm).astype(jnp.int32).reshape(1)
    return src, dst, lo_tab[tile_cls], hi_tab[tile_cls], n_used


def _gather_rows_kernel(src_ref, nused_ref, x_hbm, o_ref, sem, *, tm):
    i = pl.program_id(0)

    @pl.when(i < nused_ref[0])
    def _():
        def issue(r, carry):
            pltpu.make_async_copy(x_hbm.at[pl.ds(src_ref[i * tm + r], 1)], o_ref.at[pl.ds(r, 1)], sem.at[0]).start()
            return carry
        lax.fori_loop(0, tm, issue, 0, unroll=8)
        pltpu.make_async_copy(x_hbm.at[pl.ds(0, tm)], o_ref, sem.at[0]).wait()

    @pl.when(i >= nused_ref[0])
    def _():
        o_ref[...] = jnp.zeros_like(o_ref)


def _scatter_rows_kernel(dst_ref, nused_ref, y_ref, o_hbm, zero_ref, sem, *, tm, n):
    i = pl.program_id(0)

    @pl.when(i == 0)
    def _():
        zero_ref[...] = jnp.zeros_like(zero_ref)
        spare = pltpu.make_async_copy(zero_ref, o_hbm.at[pl.ds(n, tm)], sem.at[0])
        spare.start()
        spare.wait()

    @pl.when(i < nused_ref[0])
    def _():
        def issue(r, carry):
            pltpu.make_async_copy(y_ref.at[pl.ds(r, 1)], o_hbm.at[pl.ds(dst_ref[i * tm + r], 1)], sem.at[0]).start()
            return carry
        lax.fori_loop(0, tm, issue, 0, unroll=8)
        pltpu.make_async_copy(y_ref, o_hbm.at[pl.ds(0, tm)], sem.at[0]).wait()


def _expert_pair_kernel(lo_ref, hi_ref, nused_ref, xs_ref, wg_lo, wu_lo, wd_lo, wg_hi, wu_hi, wd_hi, y_ref, *, d):
    i = pl.program_id(0)

    @pl.when(i < nused_ref[0])
    def _():
        x16 = xs_ref[:, :d].astype(BF16)
        y = jnp.zeros(y_ref.shape, F32)
        for off, (wg, wu, wd) in ((2, (wg_lo, wu_lo, wd_lo)), (3, (wg_hi, wu_hi, wd_hi))):
            gate = xs_ref[:, d + ROUTE_LANE0 + off:d + ROUTE_LANE0 + off + 1]
            hg = _mm(x16, wg[0])
            hu = _mm(x16, wu[0])
            y = y + _mm(((hg * _sigmoid(hg)) * hu * gate).astype(BF16), wd[0])
        y_ref[...] = y

    @pl.when(i >= nused_ref[0])
    def _():
        y_ref[...] = jnp.zeros_like(y_ref)


def _add_ln_kernel(x_ref, y_ref, g_ref, b_ref, o_ref):
    o_ref[...] = _layer_norm_rows(DN_ALPHA * x_ref[...] + y_ref[...], g_ref[...], b_ref[...])


def _moe_ln_sparse(x, gates, xr, wg, wu, wd, g, b, tm_ln):
    n, d = x.shape
    tm = MOE_TM
    f = wg.shape[2]
    cls = gates[:, ROUTE_LANE0 + 4].astype(jnp.int32)
    src, dst, tile_lo, tile_hi, n_used = _route_plan(cls, n, tm)
    n_rows = src.shape[0]
    n_tiles = n_rows // tm
    xs = pl.pallas_call(
        functools.partial(_gather_rows_kernel, tm=tm),
        grid_spec=pltpu.PrefetchScalarGridSpec(
            num_scalar_prefetch=2, grid=(n_tiles,),
            in_specs=[pl.BlockSpec(memory_space=pl.ANY)],
            out_specs=pl.BlockSpec((tm, d + LANES), lambda i, s, u: (i, 0)),
            scratch_shapes=[pltpu.SemaphoreType.DMA((1,))]),
        out_shape=jax.ShapeDtypeStruct((n_rows, d + LANES), F32),
        compiler_params=_cparams(("arbitrary",)),
        name="moe_gather",
    )(src, n_used, xr)
    wspec = lambda which, shape: pl.BlockSpec(
        (1,) + shape, (lambda i, lo, hi, u: (lo[i], 0, 0)) if which == 0 else (lambda i, lo, hi, u: (hi[i], 0, 0)))
    ys = pl.pallas_call(
        functools.partial(_expert_pair_kernel, d=d),
        grid_spec=pltpu.PrefetchScalarGridSpec(
            num_scalar_prefetch=3, grid=(n_tiles,),
            in_specs=[pl.BlockSpec((tm, d + LANES), lambda i, lo, hi, u: (i, 0)),
                      wspec(0, (d, f)), wspec(0, (d, f)), wspec(0, (f, d)),
                      wspec(1, (d, f)), wspec(1, (d, f)), wspec(1, (f, d))],
            out_specs=pl.BlockSpec((tm, d), lambda i, lo, hi, u: (i, 0))),
        out_shape=jax.ShapeDtypeStruct((n_rows, d), F32),
        compiler_params=_cparams(("arbitrary",)),
        name="moe_experts",
    )(tile_lo, tile_hi, n_used, xs, wg, wu, wd, wg, wu, wd)
    y_tok = pl.pallas_call(
        functools.partial(_scatter_rows_kernel, tm=tm, n=n),
        grid_spec=pltpu.PrefetchScalarGridSpec(
            num_scalar_prefetch=2, grid=(n_tiles,),
            in_specs=[pl.BlockSpec((tm, d), lambda i, s, u: (i, 0))],
            out_specs=pl.BlockSpec(memory_space=pl.ANY),
            scratch_shapes=[pltpu.VMEM((tm, d), F32), pltpu.SemaphoreType.DMA((1,))]),
        out_shape=jax.ShapeDtypeStruct((n + tm, d), F32),
        compiler_params=_cparams(("arbitrary",)),
        name="moe_scatter",
    )(dst, n_used, ys)
    tm_ln = min(tm_ln, n)
    return pl.pallas_call(
        _add_ln_kernel,
        grid=(n // tm_ln,),
        in_specs=[pl.BlockSpec((tm_ln, d), lambda i: (i, 0)),
                  pl.BlockSpec((tm_ln, d), lambda i: (i, 0)),
                  pl.BlockSpec((1, d), lambda i: (0, 0)),
                  pl.BlockSpec((1, d), lambda i: (0, 0))],
        out_specs=pl.BlockSpec((tm_ln, d), lambda i: (i, 0)),
        out_shape=jax.ShapeDtypeStruct((n, d), F32),
        compiler_params=_cparams(("parallel",)),
        name="moe_add_ln",
    )(x, y_tok, g.reshape(1, d), b.reshape(1, d))


EVEN_WIDTHS = (SB_WIDTH, SB_WIDTH, SB_WIDTH, 3 * GDN_WIDTH, GDN_WIDTH, LANES)
ODD_WIDTHS = (ML_QK_WIDTH, ML_QK_WIDTH, ML_V_WIDTH, ML_V_WIDTH, LANES)
ROW_TILE = 512


def _pad_cols(w, total):
    return jnp.pad(w, ((0, 0), (0, total - w.shape[1])))


def _pad_tokens(a, t_pad):
    return jnp.pad(a, ((0, 0), (0, t_pad - a.shape[1])) + ((0, 0),) * (a.ndim - 2))


def _run_trunk(x, past, conv_buf, gdn_state, ml_state, p):
    b, t, d = x.shape
    n = b * t
    t_pad = max(t, CHUNK)
    xt = x.reshape(n, d)

    q_sb, k_sb, v_sb, qkv_g, z_g, ab = _project(xt, p['even_w_in'], EVEN_WIDTHS, ROW_TILE)
    q3, k3, v3 = (a.reshape(b, t, SB_WIDTH) for a in (q_sb, k_sb, v_sb))
    if past is None:
        o_sb = _sb_prompt(q3, k3, v3, p['sb_bias'])
    else:
        o_sb = _sb_sample(q3, k3, v3, past[0], past[1], past[2], p['sb_bias'])
    qkv3 = qkv_g.reshape(b, t, 3 * GDN_WIDTH)
    conv_prev8 = jnp.pad(conv_buf, ((0, 0), (8 - (GDN_CONV - 1), 0), (0, 0)))
    o_g, s_out = _gdn(_pad_tokens(qkv3, t_pad), _pad_tokens(z_g.reshape(b, t, GDN_WIDTH), t_pad),
                      _pad_tokens(ab.reshape(b, t, LANES), t_pad), conv_prev8, gdn_state,
                      p['gdn_conv_w'], p['gdn_a_log'], p['gdn_dt_bias'], p['gdn_norm_w'], t)
    new_buf = jnp.concatenate([conv_buf, qkv3], axis=1)[:, -(GDN_CONV - 1):]
    merged = jnp.concatenate([o_sb, o_g[:, :t]], axis=-1).reshape(n, SB_WIDTH + GDN_WIDTH)
    xt = _outproj_ln(merged, p['even_w_out'], xt, p['ln_mix_g'][0], p['ln_mix_b'][0], ROW_TILE)
    xt = _ffn(xt, p, 0)

    q_m, k_m, v_m, o_m, ifg = _project(xt, p['odd_w_in'], ODD_WIDTHS, ROW_TILE)
    c0, n0, m0 = ml_state
    h_m, c_out, n_out, m_out = _mlstm(
        _pad_tokens(q_m.reshape(b, t, ML_QK_WIDTH), t_pad), _pad_tokens(k_m.reshape(b, t, ML_QK_WIDTH), t_pad),
        _pad_tokens(v_m.reshape(b, t, ML_V_WIDTH), t_pad), _pad_tokens(o_m.reshape(b, t, ML_V_WIDTH), t_pad),
        _pad_tokens(ifg.reshape(b, t, LANES), t_pad), c0, n0, m0,
        p['mlstm_i_bias'], p['mlstm_f_bias'], p['mlstm_norm_w'], t)
    xt = _outproj_ln(h_m[:, :t].reshape(n, ML_V_WIDTH), p['odd_w_out'], xt,
                     p['ln_mix_g'][1], p['ln_mix_b'][1], ROW_TILE)
    xt = _ffn(xt, p, 1)

    kv_shape = (1, b, t, SB_HEADS, SB_HEAD_DIM)
    return (xt.reshape(b, t, d), k_sb.reshape(kv_shape), v_sb.reshape(kv_shape), new_buf[None], s_out[None],
            c_out[None], n_out[None], m_out[None])


def _ffn(xt, p, layer):
    sparse = xt.shape[0] >= SPARSE_MIN_TOKENS
    routed = _router(xt, p['moe_w_group'][layer], p['moe_b_group'][layer], p['moe_w_fine'][layer],
                     p['moe_b_fine'][layer], ROW_TILE, sparse)
    weights = (p['moe_w_gate'][layer], p['moe_w_up'][layer], p['moe_w_down'][layer],
               p['ln_ffn_g'][layer], p['ln_ffn_b'][layer], ROW_TILE)
    if sparse:
        return _moe_ln_sparse(xt, routed[0], routed[1], *weights)
    return _moe_ln(xt, routed[0], *weights)


def kernel(x_prompt, x_sample, cache_k, cache_v, state_conv, state_gdn, state_mlstm_C, state_mlstm_n,
           state_mlstm_m, page_table, even_w_in, even_w_out, sb_bias, gdn_conv_w, gdn_a_log, gdn_dt_bias,
           gdn_norm_w, odd_w_in, odd_w_out, mlstm_i_bias, mlstm_f_bias, mlstm_norm_w, ln_mix_g, ln_mix_b,
           ln_ffn_g, ln_ffn_b, moe_w_group, moe_b_group, moe_w_fine, moe_b_fine, moe_w_gate, moe_w_up,
           moe_w_down):
    assert DEPTH == 2 and even_w_in.shape[0] == 1 and odd_w_in.shape[0] == 1
    p = {
        'even_w_in': _pad_cols(even_w_in[0], sum(EVEN_WIDTHS)).astype(BF16),
        'even_w_out': even_w_out[0].astype(BF16),
        'sb_bias': sb_bias[0], 'gdn_conv_w': gdn_conv_w[0], 'gdn_a_log': gdn_a_log[0],
        'gdn_dt_bias': gdn_dt_bias[0], 'gdn_norm_w': gdn_norm_w[0],
        'odd_w_in': _pad_cols(odd_w_in[0], sum(ODD_WIDTHS)).astype(BF16),
        'odd_w_out': odd_w_out[0].astype(BF16),
        'mlstm_i_bias': mlstm_i_bias[0], 'mlstm_f_bias': mlstm_f_bias[0], 'mlstm_norm_w': mlstm_norm_w[0],
        'ln_mix_g': ln_mix_g, 'ln_mix_b': ln_mix_b, 'ln_ffn_g': ln_ffn_g, 'ln_ffn_b': ln_ffn_b,
        'moe_w_group': moe_w_group, 'moe_b_group': moe_b_group, 'moe_w_fine': moe_w_fine,
        'moe_b_fine': moe_b_fine, 'moe_w_gate': moe_w_gate.astype(BF16), 'moe_w_up': moe_w_up.astype(BF16),
        'moe_w_down': moe_w_down.astype(BF16),
    }
    bp = x_prompt.shape[0]
    zero_buf = jnp.zeros((bp, GDN_CONV - 1, 3 * GDN_WIDTH), F32)
    zero_s = jnp.zeros((bp, GDN_HEADS, GDN_HEAD_DIM, GDN_HEAD_DIM), F32)
    zero_ml = (jnp.zeros((bp, ML_HEADS, ML_QK_DIM, ML_V_DIM), F32), jnp.zeros((bp, ML_HEADS, ML_QK_DIM), F32),
               jnp.zeros((bp, ML_HEADS), F32))
    (y_p, k_p, v_p, conv_p, gdn_p, mc_p, mn_p, mm_p) = _run_trunk(x_prompt, None, zero_buf, zero_s, zero_ml, p)

    pool_k = jnp.transpose(cache_k.reshape(cache_k.shape[1:]), (0, 2, 3, 1))
    pool_v = jnp.transpose(cache_v.reshape(cache_v.shape[1:]), (0, 2, 3, 1))
    (y_s, k_s, v_s, conv_s, gdn_s, mc_s, mn_s, mm_s) = _run_trunk(
        x_sample, (pool_k, pool_v, page_table), state_conv[0], state_gdn[0],
        (state_mlstm_C[0], state_mlstm_n[0], state_mlstm_m[0]), p)
    return (y_p, y_s, k_p, v_p, k_s, v_s, conv_p, conv_s, gdn_p, gdn_s, mc_p, mc_s, mn_p, mn_s, mm_p, mm_s)
```

```python
import functools
import math

import jax
import jax.numpy as jnp
from jax import lax
from jax.experimental import pallas as pl
from jax.experimental.pallas import tpu as pltpu

F32 = jnp.float32
BF16 = jnp.bfloat16

D_MODEL = 1024
DEPTH = 2
PAGE_SIZE = 128
SB_HEADS = 8
SB_HEAD_DIM = 64
SB_WIDTH = SB_HEADS * SB_HEAD_DIM
SB_SCALE = SB_HEAD_DIM ** -0.5
GDN_HEADS = 4
GDN_HEAD_DIM = 128
GDN_WIDTH = GDN_HEADS * GDN_HEAD_DIM
GDN_CONV = 4
ML_HEADS = 8
ML_QK_DIM = 64
ML_V_DIM = 128
ML_QK_WIDTH = ML_HEADS * ML_QK_DIM
ML_V_WIDTH = ML_HEADS * ML_V_DIM
N_GROUPS = 4
EXPERTS_PER_GROUP = 4
N_EXPERTS = N_GROUPS * EXPERTS_PER_GROUP
EXPERT_FF = 512
DN_ALPHA = (2 * DEPTH) ** 0.25
LN_EPS = 1e-5
RMS_EPS = 1e-6
L2_EPS = 1e-6

LANES = 128
CHUNK = 64
NEG_BIG = -1e30
VMEM_LIMIT = 56 * 1024 * 1024


def _cparams(sem):
    return pltpu.CompilerParams(dimension_semantics=sem, vmem_limit_bytes=VMEM_LIMIT)


def _mm(a, b):
    return jnp.dot(a, b, preferred_element_type=F32)


def _mm_nt(a, b):
    return lax.dot_general(a, b, (((1,), (1,)), ((), ())), preferred_element_type=F32)


def _mm_tn(a, b):
    return lax.dot_general(a, b, (((0,), (0,)), ((), ())), preferred_element_type=F32)


def _split3(x):
    x1 = x.astype(BF16)
    r1 = x - x1.astype(F32)
    x2 = r1.astype(BF16)
    x3 = (r1 - x2.astype(F32)).astype(BF16)
    return x1, x2, x3


def _mm_exact_lhs(a_bf16, b):
    b1, b2, b3 = _split3(b)
    return _mm(a_bf16, b1) + _mm(a_bf16, b2) + _mm(a_bf16, b3)


def _mm3(a, b):
    a1, a2, _ = _split3(a)
    b1, b2, _ = _split3(b)
    return _mm(a1, b1) + (_mm(a1, b2) + _mm(a2, b1))


def _softplus(x):
    return jnp.maximum(x, 0.0) + jnp.log1p(jnp.exp(-jnp.abs(x)))


def _sigmoid(x):
    return jax.nn.sigmoid(x)


def _iota(shape, dim):
    return lax.broadcasted_iota(jnp.int32, shape, dim)


def _proj_kernel(x_ref, w_ref, *out_refs, widths):
    x = x_ref[...].astype(BF16)
    off = 0
    for o_ref, wd in zip(out_refs, widths):
        o_ref[...] = _mm(x, w_ref[:, off:off + wd])
        off += wd


def _project(x, w_bf16, widths, tm):
    n, d = x.shape
    tm = min(tm, n)
    total = sum(widths)
    return pl.pallas_call(
        functools.partial(_proj_kernel, widths=tuple(widths)),
        grid=(n // tm,),
        in_specs=[pl.BlockSpec((tm, d), lambda i: (i, 0)),
                  pl.BlockSpec((d, total), lambda i: (0, 0))],
        out_specs=[pl.BlockSpec((tm, wd), lambda i: (i, 0)) for wd in widths],
        out_shape=[jax.ShapeDtypeStruct((n, wd), F32) for wd in widths],
        compiler_params=_cparams(("parallel",)),
        name="proj",
    )(x, w_bf16)


def _layer_norm_rows(r, g, b):
    mu = jnp.mean(r, axis=-1, keepdims=True)
    c = r - mu
    var = jnp.mean(c * c, axis=-1, keepdims=True)
    return c * lax.rsqrt(var + LN_EPS) * g + b


def _outproj_ln_kernel(m_ref, w_ref, x_ref, g_ref, b_ref, o_ref):
    y = _mm(m_ref[...].astype(BF16), w_ref[...])
    o_ref[...] = _layer_norm_rows(DN_ALPHA * x_ref[...] + y, g_ref[...], b_ref[...])


def _outproj_ln(merged, w_bf16, x, g, b, tm):
    n, k = merged.shape
    d = x.shape[1]
    tm = min(tm, n)
    return pl.pallas_call(
        _outproj_ln_kernel,
        grid=(n // tm,),
        in_specs=[pl.BlockSpec((tm, k), lambda i: (i, 0)),
                  pl.BlockSpec((k, d), lambda i: (0, 0)),
                  pl.BlockSpec((tm, d), lambda i: (i, 0)),
                  pl.BlockSpec((1, d), lambda i: (0, 0)),
                  pl.BlockSpec((1, d), lambda i: (0, 0))],
        out_specs=pl.BlockSpec((tm, d), lambda i: (i, 0)),
        out_shape=jax.ShapeDtypeStruct((n, d), F32),
        compiler_params=_cparams(("parallel",)),
        name="outproj_ln",
    )(merged, w_bf16, x, g.reshape(1, d), b.reshape(1, d))


SB_SUB = 256
SB_TQ = 512
SB_TKB = 512


LOG2E = math.log2(math.e)


def _suffix_sum_matrix(n):
    return (jnp.arange(n)[:, None] > jnp.arange(n)[None, :]).astype(BF16)


def _sb_chains(chains, u):
    logs = [[jnp.minimum(nz, 0.0) - jnp.log(1.0 + jnp.exp2(jnp.abs(nz) * -LOG2E)) for nz in tiles]
            for tiles, _, _ in chains]
    masked = [[l if c is None else jnp.where(c, l, 0.0) for l, c in zip(ls, causals)]
              for ls, (_, causals, _) in zip(logs, chains)]
    sums = [[_mm(lm.astype(BF16), u) for lm in lms] for lms in masked]
    out = []
    for ls, lms, ss, (tiles, causals, r_b) in zip(logs, masked, sums, chains):
        ws = []
        for nz, l, lm, s, c in zip(tiles, ls, lms, ss, causals):
            tk = nz.shape[1]
            tail = s + jnp.concatenate([r_b] * (tk // LANES), axis=1)
            w = jnp.exp2(((l - nz) + tail) * LOG2E)
            ws.append(w if c is None else jnp.where(c, w, 0.0))
            r_b = r_b + jnp.broadcast_to(s[:, :1] + lm[:, :1], r_b.shape)
        out.append((ws, r_b))
    return out


def _sb_prompt_kernel(qi_ref, kb_ref, bias_ref, u_ref, q_ref, k_ref, v_ref, o_ref, acc_ref, r_ref,
                      *, tq, tkb, sub):
    hp = pl.program_id(1)
    p = pl.program_id(2)
    q0 = qi_ref[p] * tq
    k0 = kb_ref[p] * tkb

    @pl.when(k0 + tkb >= q0 + tq)
    def _():
        acc_ref[...] = jnp.zeros_like(acc_ref)
        r_ref[...] = jnp.zeros_like(r_ref)

    def body(masked):
        lane = _iota((1, LANES), 1)
        nq2 = q_ref[0] * (-SB_SCALE)
        u = u_ref[...]
        ones_blk = jnp.where(_iota((sub, LANES), 1) < 3, 1.0, 0.0).astype(BF16)
        subs = range(tkb // sub - 1, -1, -1)
        k_ext = [jnp.concatenate([k_ref[0, s * sub:(s + 1) * sub, :].astype(BF16), ones_blk], axis=1) for s in subs]
        causals = [None] * len(subs)
        if masked:
            causals = [(_iota((tq, sub), 1) + (k0 + s * sub)) < (_iota((tq, sub), 0) + q0) for s in subs]
        chains, in_heads = [], []
        for e in range(2):
            in_head = (lane >= SB_HEAD_DIM * e) & (lane < SB_HEAD_DIM * (e + 1))
            nqh = jnp.where(in_head, nq2, 0.0).astype(BF16)
            nb = [part.astype(F32) for part in _split3(jnp.full((1, LANES), -bias_ref[2 * hp + e], F32))]
            bias_row = jnp.where(lane == 0, nb[0], jnp.where(lane == 1, nb[1], jnp.where(lane == 2, nb[2], 0.0)))
            nq_ext = jnp.concatenate([nqh, jnp.broadcast_to(bias_row, (tq, LANES)).astype(BF16)], axis=1)
            chains.append(([_mm_nt(nq_ext, ks) for ks in k_ext], causals, r_ref[e]))
            in_heads.append(in_head)
        upd = jnp.zeros((tq, LANES), F32)
        for e, (ws, r_b) in enumerate(_sb_chains(chains, u)):
            r_ref[e] = r_b
            for w, s in zip(ws, subs):
                vs = jnp.where(in_heads[e], v_ref[0, s * sub:(s + 1) * sub, :], 0.0).astype(BF16)
                upd = upd + _mm(w.astype(BF16), vs)
        acc_ref[...] += upd

    needs_mask = k0 + tkb > q0

    @pl.when(needs_mask)
    def _():
        body(True)

    @pl.when(jnp.logical_not(needs_mask))
    def _():
        body(False)

    @pl.when(k0 == 0)
    def _():
        o_ref[0] = acc_ref[...]


def _sb_prompt(q, k, v, bias):
    b, t, _ = q.shape
    tq, tkb = min(SB_TQ, t), min(SB_TKB, t)
    sub = min(SB_SUB, tkb)
    qi_list, kb_list = [], []
    for i in range(t // tq):
        for kb in range((i + 1) * tq // tkb - 1, -1, -1):
            qi_list.append(i)
            kb_list.append(kb)
    qi = jnp.asarray(qi_list, jnp.int32)
    kb = jnp.asarray(kb_list, jnp.int32)
    grid_spec = pltpu.PrefetchScalarGridSpec(
        num_scalar_prefetch=2,
        grid=(b, SB_HEADS // 2, len(qi_list)),
        in_specs=[pl.BlockSpec(memory_space=pltpu.SMEM),
                  pl.BlockSpec((sub, sub), lambda bb, hp, p, qi, kb: (0, 0)),
                  pl.BlockSpec((1, tq, LANES), lambda bb, hp, p, qi, kb: (bb, qi[p], hp)),
                  pl.BlockSpec((1, tkb, LANES), lambda bb, hp, p, qi, kb: (bb, kb[p], hp)),
                  pl.BlockSpec((1, tkb, LANES), lambda bb, hp, p, qi, kb: (bb, kb[p], hp))],
        out_specs=pl.BlockSpec((1, tq, LANES), lambda bb, hp, p, qi, kb: (bb, qi[p], hp)),
        scratch_shapes=[pltpu.VMEM((tq, LANES), F32), pltpu.VMEM((2, tq, LANES), F32)],
    )
    return pl.pallas_call(
        functools.partial(_sb_prompt_kernel, tq=tq, tkb=tkb, sub=sub),
        grid_spec=grid_spec,
        out_shape=jax.ShapeDtypeStruct((b, t, SB_WIDTH), F32),
        compiler_params=_cparams(("parallel", "parallel", "arbitrary")),
        name="sb_prompt",
    )(qi, kb, bias, _suffix_sum_matrix(sub), q, k, v)


PAGES_PER_STEP = 8


def _sb_sample_kernel(pt_ref, bias_ref, u_ref, q_ref, kn_ref, vn_ref, *rest, t_new):
    k_refs = rest[:PAGES_PER_STEP]
    v_refs = rest[PAGES_PER_STEP:2 * PAGES_PER_STEP]
    o_ref, nq_ref, nb_ref, acc_ref, r_ref = rest[2 * PAGES_PER_STEP:]
    rows = SB_HEADS * t_new
    j = pl.program_id(1)

    def process(page_k_refs, page_v_refs, masked):
        n_pg = len(page_k_refs)
        keys = lambda refs, h: jnp.concatenate([ref[0, h].astype(BF16) for ref in refs], axis=1)
        nz = jnp.concatenate([_mm(nq_ref[h], keys(page_k_refs, h)) for h in range(SB_HEADS)], axis=0)
        nz = nz + jnp.concatenate([nb_ref[...]] * n_pg, axis=1)
        causal = None
        if masked:
            causal = _iota((rows, PAGE_SIZE), 1) < (_iota((rows, PAGE_SIZE), 0) % t_new)
        tiles = [nz[:, g * PAGE_SIZE:(g + 1) * PAGE_SIZE] for g in range(n_pg)]
        (ws, r_b), = _sb_chains([(tiles, [causal] * n_pg, r_ref[...])], u_ref[...])
        r_ref[...] = r_b
        w = jnp.concatenate(ws, axis=1)
        for h in range(SB_HEADS):
            acc_ref[h] += _mm_nt(w[h * t_new:(h + 1) * t_new].astype(BF16), keys(page_v_refs, h))

    @pl.when(j == 0)
    def _():
        nq_ref[...] = (q_ref[0] * (-SB_SCALE)).astype(BF16)
        hrow = _iota((rows, LANES), 0) // t_new
        nb = jnp.zeros((rows, LANES), F32)
        for h in range(SB_HEADS):
            nb = jnp.where(hrow == h, -bias_ref[h], nb)
        nb_ref[...] = nb
        acc_ref[...] = jnp.zeros_like(acc_ref)
        r_ref[...] = jnp.zeros_like(r_ref)
        process([kn_ref], [vn_ref], True)

    process(k_refs, v_refs, False)

    @pl.when(j == pl.num_programs(1) - 1)
    def _():
        o_ref[0] = acc_ref[...]


def _sb_sample(q, k_new, v_new, pool_k, pool_v, page_table, bias):
    b, t_new, _ = q.shape
    n_pages = page_table.shape[1]
    steps = n_pages // PAGES_PER_STEP
    rows = SB_HEADS * t_new
    page_block = (1, SB_HEADS, SB_HEAD_DIM, PAGE_SIZE)

    def as_page(a):
        a = a.reshape(b, t_new, SB_HEADS, SB_HEAD_DIM).transpose(0, 2, 3, 1)
        return jnp.pad(a, ((0, 0), (0, 0), (0, 0), (0, PAGE_SIZE - t_new)))

    q_heads = q.reshape(b, t_new, SB_HEADS, SB_HEAD_DIM).transpose(0, 2, 1, 3)

    def page_spec(r):
        def imap(bb, j, pt):
            return (pt[bb, n_pages - 1 - (j * PAGES_PER_STEP + r)], 0, 0, 0)
        return pl.BlockSpec(page_block, imap)

    new_spec = pl.BlockSpec(page_block, lambda bb, j, pt: (bb, 0, 0, 0))
    qo_spec = pl.BlockSpec((1, SB_HEADS, t_new, SB_HEAD_DIM), lambda bb, j, pt: (bb, 0, 0, 0))
    grid_spec = pltpu.PrefetchScalarGridSpec(
        num_scalar_prefetch=1,
        grid=(b, steps),
        in_specs=([pl.BlockSpec(memory_space=pltpu.SMEM),
                   pl.BlockSpec((PAGE_SIZE, PAGE_SIZE), lambda bb, j, pt: (0, 0)),
                   qo_spec, new_spec, new_spec]
                  + [page_spec(r) for r in range(PAGES_PER_STEP)]
                  + [page_spec(r) for r in range(PAGES_PER_STEP)]),
        out_specs=qo_spec,
        scratch_shapes=[pltpu.VMEM((SB_HEADS, t_new, SB_HEAD_DIM), BF16), pltpu.VMEM((rows, LANES), F32),
                        pltpu.VMEM((SB_HEADS, t_new, SB_HEAD_DIM), F32), pltpu.VMEM((rows, LANES), F32)],
    )
    o = pl.pallas_call(
        functools.partial(_sb_sample_kernel, t_new=t_new),
        grid_spec=grid_spec,
        out_shape=jax.ShapeDtypeStruct((b, SB_HEADS, t_new, SB_HEAD_DIM), F32),
        compiler_params=_cparams(("parallel", "arbitrary")),
        name="sb_sample",
    )(page_table, bias, _suffix_sum_matrix(PAGE_SIZE), q_heads, as_page(k_new), as_page(v_new),
      *([pool_k] * PAGES_PER_STEP), *([pool_v] * PAGES_PER_STEP))
    return o.transpose(0, 2, 1, 3).reshape(b, t_new, SB_WIDTH)


def _chunk_masks():
    i = _iota((CHUNK, CHUNK), 0)
    j = _iota((CHUNK, CHUNK), 1)
    return i >= j, i > j


def _gdn_prep_kernel(alog_ref, dtb_ref, qr_ref, kr_ref, vr_ref, qp_ref, kp_ref, vp_ref,
                     qs_ref, ks_ref, vs_ref, wq_ref, wk_ref, wv_ref, ab_ref,
                     u_ref, w_ref, qg_ref, kd_ref, sd_ref, eg_ref, xbuf_ref, *, tc, t_valid):
    h = pl.program_id(1)
    c = pl.program_id(2)

    def conv_silu(x_ref, prev_ref, state_ref, wt_ref):
        first = (c == 0).astype(F32)
        xbuf_ref[0:8, :] = first * state_ref[0] + (1.0 - first) * prev_ref[0]
        xbuf_ref[8:, :] = x_ref[0]
        y = jnp.zeros((tc, LANES), F32)
        for i in range(GDN_CONV):
            y = y + xbuf_ref[5 + i:5 + i + tc, :] * wt_ref[i:i + 1, :]
        return y * _sigmoid(y)

    qc = conv_silu(qr_ref, qp_ref, qs_ref, wq_ref)
    kc = conv_silu(kr_ref, kp_ref, ks_ref, wk_ref)
    vc = conv_silu(vr_ref, vp_ref, vs_ref, wv_ref)
    q = qc * lax.rsqrt(jnp.sum(qc * qc, axis=-1, keepdims=True) + L2_EPS) * GDN_HEAD_DIM ** -0.5
    k = kc * lax.rsqrt(jnp.sum(kc * kc, axis=-1, keepdims=True) + L2_EPS)

    ab = ab_ref[0]
    lane = _iota((tc, LANES), 1)
    a_b = jnp.broadcast_to(jnp.sum(jnp.where(lane == h, ab, 0.0), axis=-1, keepdims=True), (tc, LANES))
    b_b = jnp.broadcast_to(jnp.sum(jnp.where(lane == GDN_HEADS + h, ab, 0.0), axis=-1, keepdims=True),
                           (tc, LANES))
    neg_a = -jnp.exp(jnp.full((1, LANES), alog_ref[h], F32))
    g_b = neg_a * _softplus(a_b + dtb_ref[h])
    beta_b = _sigmoid(b_b)
    valid = (_iota((tc, LANES), 0) + c * tc) < t_valid
    g_b = jnp.where(valid, g_b, 0.0)
    beta_b = jnp.where(valid, beta_b, 0.0)

    lower, strict = _chunk_masks()
    t1 = lower.astype(BF16)
    t2 = (_iota((CHUNK, CHUNK), 0) > _iota((CHUNK, CHUNK), 1)).astype(F32)
    eye = (_iota((CHUNK, CHUNK), 0) == _iota((CHUNK, CHUNK), 1)).astype(F32)

    chunks = range(tc // CHUNK)
    sls = [slice(n * CHUNK, (n + 1) * CHUNK) for n in chunks]
    gc_b = [_mm_exact_lhs(t1, g_b[sl]) for sl in sls]
    dg = [_mm_exact_lhs(t1, g_b[sl][:, :CHUNK] * t2) for sl in sls]
    decay = [jnp.where(lower, jnp.exp(jnp.where(lower, d, 0.0)), 0.0) for d in dg]
    kbeta = [k[sl] * beta_b[sl] for sl in sls]
    k16 = [k[sl].astype(BF16) for sl in sls]
    a = [jnp.where(strict, _mm_nt(kbeta[n].astype(BF16), k16[n]) * decay[n], 0.0) for n in chunks]
    tinv = [eye - a[n] for n in chunks]
    pw = a
    for _ in range(int(math.log2(CHUNK)) - 1):
        pw = [_mm3(pw[n], pw[n]) for n in chunks]
        tinv = [tinv[n] + _mm3(tinv[n], pw[n]) for n in chunks]
    sol = [_mm3(tinv[n], jnp.concatenate([vc[sls[n]] * beta_b[sls[n]], kbeta[n] * jnp.exp(gc_b[n])], axis=1))
           for n in chunks]
    for n in chunks:
        sl = sls[n]
        u_ref[0, 0, sl, :] = sol[n][:, :GDN_HEAD_DIM]
        w_ref[0, 0, sl, :] = sol[n][:, GDN_HEAD_DIM:].astype(BF16)
        sd_ref[0, 0, sl, :] = (_mm_nt(q[sl].astype(BF16), k16[n]) * decay[n]).astype(BF16)
        qg_ref[0, 0, sl, :] = (q[sl] * jnp.exp(gc_b[n])).astype(BF16)
        g_last = gc_b[n][CHUNK - 1:CHUNK, :]
        kd_ref[0, 0, sl, :] = (k[sl] * jnp.exp(g_last - gc_b[n])).astype(BF16)
        eg_ref[0, 0, n:n + 1, :] = jnp.exp(g_last)


def _gdn_scan_kernel(u_ref, w_ref, qg_ref, kd_ref, sd_ref, eg_ref, z_ref, nw_ref, s0_ref,
                     o_ref, sout_ref, s_ref, *, tc):
    c = pl.program_id(2)

    @pl.when(c == 0)
    def _():
        s_ref[...] = s0_ref[0, 0]

    s = s_ref[...]
    for n in range(tc // CHUNK):
        sl = slice(n * CHUNK, (n + 1) * CHUNK)
        s16 = s.astype(BF16)
        v_new = u_ref[0, 0, sl, :] - _mm(w_ref[0, 0, sl, :], s16)
        vn16 = v_new.astype(BF16)
        o = _mm(qg_ref[0, 0, sl, :], s16) + _mm(sd_ref[0, 0, sl, :], vn16)
        s = eg_ref[0, 0, n:n + 1, :] * s + _mm_tn(kd_ref[0, 0, sl, :], vn16)
        zg = z_ref[0, sl, :]
        on = o * lax.rsqrt(jnp.mean(o * o, axis=-1, keepdims=True) + RMS_EPS) * nw_ref[...]
        o_ref[0, sl, :] = on * (zg * _sigmoid(zg))
    s_ref[...] = s

    @pl.when(c == pl.num_programs(2) - 1)
    def _():
        sout_ref[0, 0] = s


def _gdn(qkv, z, ab, conv_prev8, s0, conv_w, a_log, dt_bias, norm_w, t_valid):
    b, t, _ = qkv.shape
    tc = min(512, t)
    nb = t // tc
    nch = t // CHUNK
    hd = GDN_HEAD_DIM
    col = lambda off: pl.BlockSpec((1, tc, hd), lambda bb, h, c: (bb, c, h + off))
    prev = lambda off: pl.BlockSpec((1, 8, hd), lambda bb, h, c: (bb, jnp.maximum(c * (tc // 8) - 1, 0), h + off))
    state = lambda off: pl.BlockSpec((1, 8, hd), lambda bb, h, c: (bb, 0, h + off))
    wspec = lambda off: pl.BlockSpec((GDN_CONV, hd), lambda bb, h, c: (0, h + off))
    per_head = lambda width: pl.BlockSpec((1, 1, tc, width), lambda bb, h, c: (bb, h, c, 0))
    eg_spec = pl.BlockSpec((1, 1, tc // CHUNK, hd), lambda bb, h, c: (bb, h, c, 0))
    smem = pl.BlockSpec(memory_space=pltpu.SMEM)
    offs = (0, GDN_HEADS, 2 * GDN_HEADS)
    u, w, qg, kd, sd, eg = pl.pallas_call(
        functools.partial(_gdn_prep_kernel, tc=tc, t_valid=t_valid),
        grid=(b, GDN_HEADS, nb),
        in_specs=([smem, smem] + [col(o) for o in offs] + [prev(o) for o in offs]
                  + [state(o) for o in offs] + [wspec(o) for o in offs]
                  + [pl.BlockSpec((1, tc, LANES), lambda bb, h, c: (bb, c, 0))]),
        out_specs=[per_head(hd), per_head(hd), per_head(hd), per_head(hd), per_head(CHUNK), eg_spec],
        out_shape=[jax.ShapeDtypeStruct((b, GDN_HEADS, t, hd), F32)]
                  + [jax.ShapeDtypeStruct((b, GDN_HEADS, t, hd), BF16)] * 3
                  + [jax.ShapeDtypeStruct((b, GDN_HEADS, t, CHUNK), BF16),
                     jax.ShapeDtypeStruct((b, GDN_HEADS, nch, hd), F32)],
        scratch_shapes=[pltpu.VMEM((tc + 8, hd), F32)],
        compiler_params=_cparams(("parallel", "parallel", "parallel")),
        name="gdn_prep",
    )(a_log, dt_bias, qkv, qkv, qkv, qkv, qkv, qkv, conv_prev8, conv_prev8, conv_prev8,
      conv_w, conv_w, conv_w, ab)
    og, s_out = pl.pallas_call(
        functools.partial(_gdn_scan_kernel, tc=tc),
        grid=(b, GDN_HEADS, nb),
        in_specs=[per_head(hd), per_head(hd), per_head(hd), per_head(hd), per_head(CHUNK), eg_spec,
                  pl.BlockSpec((1, tc, hd), lambda bb, h, c: (bb, c, h)),
                  pl.BlockSpec((1, hd), lambda bb, h, c: (0, 0)),
                  pl.BlockSpec((1, 1, hd, hd), lambda bb, h, c: (bb, h, 0, 0))],
        out_specs=[pl.BlockSpec((1, tc, hd), lambda bb, h, c: (bb, c, h)),
                   pl.BlockSpec((1, 1, hd, hd), lambda bb, h, c: (bb, h, 0, 0))],
        out_shape=[jax.ShapeDtypeStruct((b, t, GDN_WIDTH), F32),
                   jax.ShapeDtypeStruct((b, GDN_HEADS, hd, hd), F32)],
        scratch_shapes=[pltpu.VMEM((hd, hd), F32)],
        compiler_params=_cparams(("parallel", "parallel", "arbitrary")),
        name="gdn_scan",
    )(u, w, qg, kd, sd, eg, z, norm_w.reshape(1, hd), s0)
    return og, s_out


def _mlstm_kernel(ib_ref, fb_ref, m0_ref, q_ref, k_ref, v_ref, og_ref, if_ref, c0_ref, n0_ref, nw_ref,
                  h_ref, cout_ref, nout_ref, mout_ref, c_ref, n_ref, m_ref, *, tc, t_valid):
    bb = pl.program_id(0)
    p = pl.program_id(1)
    c = pl.program_id(2)
    vd = ML_V_DIM

    @pl.when(c == 0)
    def _():
        c_ref[...] = c0_ref[0, 0]
        n_ref[...] = n0_ref[0, 0]
        for e in range(2):
            m_ref[e] = jnp.full((1, LANES), m0_ref[bb, 2 * p + e], F32)

    lower, _ = _chunk_masks()
    t1 = lower.astype(BF16)
    t2 = (_iota((CHUNK, CHUNK), 0) > _iota((CHUNK, CHUNK), 1)).astype(F32)
    eye = (_iota((CHUNK, CHUNK), 0) == _iota((CHUNK, CHUNK), 1)).astype(F32)
    ones = jnp.ones((CHUNK, CHUNK), BF16)
    lane_row = _iota((1, LANES), 1)
    lane = _iota((tc, LANES), 1)
    gates = if_ref[0]

    def gate_col(idx):
        col = jnp.sum(jnp.where(lane == idx, gates, 0.0), axis=-1, keepdims=True)
        return jnp.broadcast_to(col, (tc, LANES))

    i_b, lf_b = [], []
    for e in range(2):
        hd = 2 * p + e
        i_b.append(gate_col(hd) + ib_ref[hd])
        lf_b.append(-_softplus(-(gate_col(ML_HEADS + hd) + fb_ref[hd])))

    chunks = range(tc // CHUNK)
    sls = [slice(n * CHUNK, (n + 1) * CHUNK) for n in chunks]
    valid = [lower & ((_iota((CHUNK, CHUNK), 1) + (c * tc + n * CHUNK)) < t_valid) for n in chunks]
    row_valid = [(_iota((CHUNK, LANES), 0) + (c * tc + n * CHUNK)) < t_valid for n in chunks]
    v16 = [[v_ref[0, sl, e * vd:(e + 1) * vd].astype(BF16) for sl in sls] for e in range(2)]
    qm, km, qm16, qk, bc_b, dm, ic = ([[None] * len(sls) for _ in range(2)] for _ in range(7))
    for e in range(2):
        in_head = (lane_row >= ML_QK_DIM * e) & (lane_row < ML_QK_DIM * (e + 1))
        for n in chunks:
            qm[e][n] = jnp.where(in_head, q_ref[0, sls[n], :], 0.0)
            km[e][n] = jnp.where(in_head, k_ref[0, sls[n], :] * ML_QK_DIM ** -0.5, 0.0)
            qm16[e][n] = qm[e][n].astype(BF16)
            qk[e][n] = _mm_nt(qm16[e][n], km[e][n].astype(BF16))
            lf = jnp.where(row_valid[n], lf_b[e][sls[n]], 0.0)
            ic[e][n] = i_b[e][sls[n]]
            bc_b[e][n] = _mm_exact_lhs(t1, lf)
            dm[e][n] = (_mm_exact_lhs(t1, lf[:, :CHUNK] * t2)
                        + _mm_exact_lhs(ones, eye * ic[e][n][:, :CHUNK]))

    m_t, m_in = ([[None] * len(sls) for _ in range(2)] for _ in range(2))
    for e in range(2):
        m_prev = m_ref[e]
        for n in chunks:
            m_in[e][n] = m_prev
            row_max = jnp.max(jnp.where(valid[n], dm[e][n], NEG_BIG), axis=-1, keepdims=True)
            m_t[e][n] = jnp.maximum(bc_b[e][n] + m_prev, row_max)
            m_prev = m_t[e][n][CHUNK - 1:CHUNK, :]
        m_ref[e] = m_prev

    s, w_state, a_state, c_upd, n_upd = ([[None] * len(sls) for _ in range(2)] for _ in range(5))
    for e in range(2):
        for n in chunks:
            mt = m_t[e][n]
            w_intra = jnp.where(valid[n], jnp.exp(jnp.where(valid[n], dm[e][n], 0.0) - mt[:, :CHUNK]), 0.0)
            w_state[e][n] = jnp.exp(bc_b[e][n] + m_in[e][n] - mt)
            s[e][n] = qk[e][n] * w_intra
            m_new = mt[CHUNK - 1:CHUNK, :]
            bc_last = bc_b[e][n][CHUNK - 1:CHUNK, :]
            a_state[e][n] = jnp.exp(bc_last + m_in[e][n] - m_new)
            wk = jnp.where(row_valid[n], jnp.exp(bc_last - bc_b[e][n] + ic[e][n] - m_new), 0.0)
            kw = km[e][n] * wk
            c_upd[e][n] = _mm_tn(kw.astype(BF16), v16[e][n])
            n_upd[e][n] = jnp.sum(kw, axis=0, keepdims=True)

    c_state, n_state = [c_ref[...]], [n_ref[...]]
    upper_rows = _iota((LANES, vd), 0) < ML_QK_DIM
    for n in chunks:
        a_rows = jnp.where(upper_rows, a_state[0][n], a_state[1][n])
        c_state.append(a_rows * c_state[n] + (c_upd[0][n] + c_upd[1][n]))
        a_lanes = jnp.where(lane_row < ML_QK_DIM, a_state[0][n], a_state[1][n])
        n_state.append(a_lanes * n_state[n] + (n_upd[0][n] + n_upd[1][n]))
    c_ref[...] = c_state[-1]
    n_ref[...] = n_state[-1]

    for n in chunks:
        cp16 = c_state[n].astype(BF16)
        for e in range(2):
            num = w_state[e][n] * _mm(qm16[e][n], cp16) + _mm(s[e][n].astype(BF16), v16[e][n])
            qn = jnp.sum(qm[e][n] * n_state[n], axis=-1, keepdims=True)
            den = w_state[e][n] * qn + jnp.sum(s[e][n], axis=-1, keepdims=True)
            hh = num / jnp.maximum(jnp.abs(den), jnp.exp(-m_t[e][n]))
            og = og_ref[0, sls[n], e * vd:(e + 1) * vd]
            hn = hh * lax.rsqrt(jnp.mean(hh * hh, axis=-1, keepdims=True) + RMS_EPS) * nw_ref[...]
            h_ref[0, sls[n], e * vd:(e + 1) * vd] = hn * _sigmoid(og)

    @pl.when(c == pl.num_programs(2) - 1)
    def _():
        cout_ref[0, 0] = c_ref[...]
        nout_ref[0, 0] = n_ref[...]
        for e in range(2):
            mout_ref[0, e] = m_ref[e]


def _mlstm(q, k, v, og, ifg, c0, n0, m0, i_bias, f_bias, norm_w, t_valid):
    b, t, _ = q.shape
    tc = min(512, t)
    nb = t // tc
    pairs = ML_HEADS // 2
    smem = pl.BlockSpec(memory_space=pltpu.SMEM)
    qk_spec = pl.BlockSpec((1, tc, LANES), lambda bb, p, c: (bb, c, p))
    v_spec = pl.BlockSpec((1, tc, 2 * ML_V_DIM), lambda bb, p, c: (bb, c, p))
    c_spec = pl.BlockSpec((1, 1, LANES, ML_V_DIM), lambda bb, p, c: (bb, p, 0, 0))
    n_spec = pl.BlockSpec((1, 1, 1, LANES), lambda bb, p, c: (bb, p, 0, 0))
    m_spec = pl.BlockSpec((1, 2, 1, LANES), lambda bb, p, c: (bb, p, 0, 0))
    h, c_out, n_out, m_out = pl.pallas_call(
        functools.partial(_mlstm_kernel, tc=tc, t_valid=t_valid),
        grid=(b, pairs, nb),
        in_specs=[smem, smem, smem, qk_spec, qk_spec, v_spec, v_spec,
                  pl.BlockSpec((1, tc, LANES), lambda bb, p, c: (bb, c, 0)),
                  c_spec, n_spec, pl.BlockSpec((1, ML_V_DIM), lambda bb, p, c: (0, 0))],
        out_specs=[v_spec, c_spec, n_spec, m_spec],
        out_shape=[jax.ShapeDtypeStruct((b, t, ML_V_WIDTH), F32),
                   jax.ShapeDtypeStruct((b, pairs, LANES, ML_V_DIM), F32),
                   jax.ShapeDtypeStruct((b, pairs, 1, LANES), F32),
                   jax.ShapeDtypeStruct((b, ML_HEADS, 1, LANES), F32)],
        scratch_shapes=[pltpu.VMEM((LANES, ML_V_DIM), F32), pltpu.VMEM((1, LANES), F32),
                        pltpu.VMEM((2, 1, LANES), F32)],
        compiler_params=_cparams(("parallel", "parallel", "arbitrary")),
        name="mlstm",
    )(i_bias, f_bias, m0, q, k, v, og, ifg,
      c0.reshape(b, pairs, LANES, ML_V_DIM), n0.reshape(b, pairs, 1, LANES), norm_w.reshape(1, ML_V_DIM))
    return (h, c_out.reshape(b, ML_HEADS, ML_QK_DIM, ML_V_DIM), n_out.reshape(b, ML_HEADS, ML_QK_DIM),
            m_out[:, :, 0, 0])


GROUP_LANE0 = N_EXPERTS
ROUTE_LANE0 = N_EXPERTS
PAIRS_PER_GROUP = EXPERTS_PER_GROUP * (EXPERTS_PER_GROUP - 1) // 2
N_CLASSES = N_GROUPS * PAIRS_PER_GROUP
MOE_TM = 256
SPARSE_MIN_TOKENS = 4096


def _router_kernel(x_ref, w_ref, b_ref, tri_ref, g_ref, n_ref, *rest):
    xr_ref, cnt_ref = rest[:-1], rest[-1]
    x1, x2, _ = _split3(x_ref[...])
    w1, w2, _ = _split3(w_ref[...])
    logits = _mm(x1, w1) + (_mm(x1, w2) + _mm(x2, w1)) + b_ref[...]
    tm = logits.shape[0]
    lane = _iota((tm, LANES), 1)
    lane_f = lane.astype(F32)
    is_group = (lane >= GROUP_LANE0) & (lane < GROUP_LANE0 + N_GROUPS)
    gl = jnp.where(is_group, logits, NEG_BIG)
    ge = jnp.where(is_group, jnp.exp(gl - jnp.max(gl, axis=-1, keepdims=True)), 0.0)
    gp = ge / jnp.sum(ge, axis=-1, keepdims=True)
    g_gate = jnp.max(gp, axis=-1, keepdims=True)
    g_idx = jnp.min(jnp.where(is_group & (gp == g_gate), lane_f, 1e9), axis=-1, keepdims=True) - GROUP_LANE0
    lo = g_idx * EXPERTS_PER_GROUP
    in_group = (lane_f >= lo) & (lane_f < lo + EXPERTS_PER_GROUP)
    fl = jnp.where(in_group, logits, NEG_BIG)
    fe = jnp.where(in_group, jnp.exp(fl - jnp.max(fl, axis=-1, keepdims=True)), 0.0)
    fp = fe / jnp.sum(fe, axis=-1, keepdims=True)
    w_a = jnp.max(jnp.where(in_group, fp, -1.0), axis=-1, keepdims=True)
    i_a = jnp.min(jnp.where(in_group & (fp == w_a), lane_f, 1e9), axis=-1, keepdims=True)
    rest = in_group & (lane_f != i_a)
    w_b = jnp.max(jnp.where(rest, fp, -1.0), axis=-1, keepdims=True)
    i_b = jnp.min(jnp.where(rest & (fp == w_b), lane_f, 1e9), axis=-1, keepdims=True)
    tot = w_a + w_b
    gate_a = g_gate * (w_a / tot)
    gate_b = g_gate * (w_b / tot)
    e_lo = jnp.minimum(i_a, i_b)
    e_hi = jnp.maximum(i_a, i_b)
    a_loc = e_lo - lo
    pair = a_loc * (7.0 - a_loc) * 0.5 + (e_hi - e_lo - 1.0)
    cls = g_idx * PAIRS_PER_GROUP + pair
    @pl.when(pl.program_id(0) == 0)
    def _():
        cnt_ref[...] = jnp.zeros_like(cnt_ref)

    onehot = jnp.where(lane_f == cls, 1.0, 0.0)
    earlier = _mm(tri_ref[...], onehot.astype(BF16)) + cnt_ref[0:1, :]
    rank = jnp.sum(onehot * earlier, axis=-1, keepdims=True)
    cnt_ref[...] = cnt_ref[...] + jnp.sum(onehot, axis=0, keepdims=True)
    n_ref[...] = cnt_ref[...]
    record = (e_lo, e_hi, jnp.where(i_a < i_b, gate_a, gate_b), jnp.where(i_a < i_b, gate_b, gate_a), cls, rank)
    out = jnp.where(lane_f == i_a, gate_a, jnp.where(lane_f == i_b, gate_b, 0.0))
    for off, val in enumerate(record):
        out = jnp.where(lane == ROUTE_LANE0 + off, val, out)
    g_ref[...] = out
    if xr_ref:
        d = x_ref.shape[1]
        xr_ref[0][:, :d] = x_ref[...]
        xr_ref[0][:, d:] = out


def _router(x, w_group, b_group, w_fine, b_fine, tm, with_rows):
    n, d = x.shape
    tm = min(tm, n)
    w = jnp.concatenate([w_fine.reshape(d, N_EXPERTS), w_group], axis=1)
    w = jnp.pad(w, ((0, 0), (0, LANES - w.shape[1])))
    bias = jnp.concatenate([b_fine.reshape(N_EXPERTS), b_group])
    bias = jnp.pad(bias, (0, LANES - bias.shape[0])).reshape(1, LANES)
    tri = (jnp.arange(tm)[:, None] > jnp.arange(tm)[None, :]).astype(BF16)
    out_specs = [pl.BlockSpec((tm, LANES), lambda i: (i, 0)), pl.BlockSpec((8, LANES), lambda i: (0, 0))]
    out_shape = [jax.ShapeDtypeStruct((n, LANES), F32), jax.ShapeDtypeStruct((8, LANES), F32)]
    if with_rows:
        out_specs.append(pl.BlockSpec((tm, d + LANES), lambda i: (i, 0)))
        out_shape.append(jax.ShapeDtypeStruct((n, d + LANES), F32))
    return pl.pallas_call(
        _router_kernel,
        grid=(n // tm,),
        in_specs=[pl.BlockSpec((tm, d), lambda i: (i, 0)),
                  pl.BlockSpec((d, LANES), lambda i: (0, 0)),
                  pl.BlockSpec((1, LANES), lambda i: (0, 0)),
                  pl.BlockSpec((tm, tm), lambda i: (0, 0))],
        out_specs=out_specs,
        out_shape=out_shape,
        scratch_shapes=[pltpu.VMEM((8, LANES), F32)],
        compiler_params=_cparams(("arbitrary",)),
        name="router",
    )(x, w, bias, tri)


def _moe_kernel(x_ref, gt_ref, wg_ref, wu_ref, wd_ref, g_ref, b_ref, o_ref, xb_ref, acc_ref):
    e = pl.program_id(1)

    @pl.when(e == 0)
    def _():
        xb_ref[...] = x_ref[...].astype(BF16)
        acc_ref[...] = jnp.zeros_like(acc_ref)

    gates = gt_ref[...]
    lane = _iota(gates.shape, 1)
    gcol = jnp.sum(jnp.where(lane == e, gates, 0.0), axis=-1, keepdims=True)
    xb = xb_ref[...]
    hg = _mm(xb, wg_ref[0])
    hu = _mm(xb, wu_ref[0])
    hh = (hg * _sigmoid(hg)) * hu * gcol
    acc_ref[...] += _mm(hh.astype(BF16), wd_ref[0])

    @pl.when(e == pl.num_programs(1) - 1)
    def _():
        o_ref[...] = _layer_norm_rows(DN_ALPHA * x_ref[...] + acc_ref[...], g_ref[...], b_ref[...])


def _moe_ln(x, gates, wg, wu, wd, g, b, tm):
    n, d = x.shape
    tm = min(tm, n)
    f = wg.shape[2]
    return pl.pallas_call(
        _moe_kernel,
        grid=(n // tm, N_EXPERTS),
        in_specs=[pl.BlockSpec((tm, d), lambda i, e: (i, 0)),
                  pl.BlockSpec((tm, LANES), lambda i, e: (i, 0)),
                  pl.BlockSpec((1, d, f), lambda i, e: (e, 0, 0)),
                  pl.BlockSpec((1, d, f), lambda i, e: (e, 0, 0)),
                  pl.BlockSpec((1, f, d), lambda i, e: (e, 0, 0)),
                  pl.BlockSpec((1, d), lambda i, e: (0, 0)),
                  pl.BlockSpec((1, d), lambda i, e: (0, 0))],
        out_specs=pl.BlockSpec((tm, d), lambda i, e: (i, 0)),
        out_shape=jax.ShapeDtypeStruct((n, d), F32),
        scratch_shapes=[pltpu.VMEM((tm, d), BF16), pltpu.VMEM((tm, d), F32)],
        compiler_params=_cparams(("parallel", "arbitrary")),
        name="moe_ln",
    )(x, gates, wg, wu, wd, g.reshape(1, d), b.reshape(1, d))


def _route_plan(cls, rank, counts, n, tm):
    n_tiles = (n + N_CLASSES * tm) // tm
    pairs = [(a, b) for a in range(EXPERTS_PER_GROUP) for b in range(a + 1, EXPERTS_PER_GROUP)]
    lo_tab = jnp.asarray([EXPERTS_PER_GROUP * g + a for g in range(N_GROUPS) for a, _ in pairs], jnp.int32)
    hi_tab = jnp.asarray([EXPERTS_PER_GROUP * g + b for g in range(N_GROUPS) for _, b in pairs], jnp.int32)
    classes = jnp.arange(N_CLASSES, dtype=jnp.int32)
    padded = (counts + tm - 1) // tm * tm
    p_end = jnp.sum(jnp.where(classes[None, :] <= classes[:, None], padded[None, :], 0), axis=1)
    p_start = p_end - padded
    pos = jnp.sum(jnp.where(cls[:, None] == classes[None, :], p_start[None, :], 0), axis=1) + rank
    last_tile_row = jnp.where(padded > 0, p_end - tm, -1)
    tile_row0 = jnp.arange(n_tiles, dtype=jnp.int32) * tm
    tile_cls = jnp.minimum(jnp.sum((p_end[None, :] <= tile_row0[:, None]).astype(jnp.int32), axis=1), N_CLASSES - 1)
    pick = lambda tab: jnp.sum(jnp.where(tile_cls[:, None] == classes[None, :], tab[None, :], 0), axis=1)
    n_used = (p_end[N_CLASSES - 1] // tm).reshape(1)
    return pos.astype(jnp.int32), last_tile_row.astype(jnp.int32), pick(lo_tab), pick(hi_tab), n_used.astype(jnp.int32)


def _row_copies(src_row, dst_row, sems, tm):
    def issue(r2, carry):
        for prio in range(2):
            r = 2 * r2 + prio
            pltpu.make_async_copy(src_row(r), dst_row(r), sems.at[prio]).start(priority=prio)
        return carry
    lax.fori_loop(0, tm // 2, issue, 0, unroll=4)
    for prio in range(2):
        pltpu.make_async_copy(src_row(0, tm // 2), dst_row(0, tm // 2), sems.at[prio]).wait()


def _dispatch_rows_kernel(pos_ref, last_ref, nused_ref, xr_ref, xs_hbm, zero_ref, sems, *, tm, n_tiles):
    i = pl.program_id(0)

    @pl.when(i == 0)
    def _():
        zero_ref[...] = jnp.zeros_like(zero_ref)

        def fill(row0):
            cp = pltpu.make_async_copy(zero_ref, xs_hbm.at[pl.ds(pl.multiple_of(row0, tm), tm)], sems.at[0])
            cp.start()
            cp.wait()

        for c in range(N_CLASSES):
            @pl.when(last_ref[c] >= 0)
            def _():
                fill(last_ref[c])

        def fill_unused(t, carry):
            fill(t * tm)
            return carry
        lax.fori_loop(nused_ref[0], n_tiles, fill_unused, 0)

    _row_copies(lambda r, rows=1: xr_ref.at[pl.ds(r, rows)],
                lambda r, rows=1: xs_hbm.at[pl.ds(pos_ref[i * tm + r] if rows == 1 else 0, rows)], sems, tm)


def _combine_ln_kernel(pos_ref, ys_hbm, x_ref, g_ref, b_ref, o_ref, buf_ref, sems, *, tm):
    i = pl.program_id(0)
    _row_copies(lambda r, rows=1: ys_hbm.at[pl.ds(pos_ref[i * tm + r] if rows == 1 else 0, rows)],
                lambda r, rows=1: buf_ref.at[pl.ds(r, rows)], sems, tm)
    o_ref[...] = _layer_norm_rows(DN_ALPHA * x_ref[...] + buf_ref[...], g_ref[...], b_ref[...])


def _expert_pair_kernel(lo_ref, hi_ref, nused_ref, xs_ref, wg_lo, wu_lo, wd_lo, wg_hi, wu_hi, wd_hi, y_ref, *, d):
    i = pl.program_id(0)

    @pl.when(i < nused_ref[0])
    def _():
        x16 = xs_ref[:, :d].astype(BF16)
        y = jnp.zeros(y_ref.shape, F32)
        for off, (wg, wu, wd) in ((2, (wg_lo, wu_lo, wd_lo)), (3, (wg_hi, wu_hi, wd_hi))):
            gate = xs_ref[:, d + ROUTE_LANE0 + off:d + ROUTE_LANE0 + off + 1]
            hg = _mm(x16, wg[0])
            hu = _mm(x16, wu[0])
            y = y + _mm(((hg * _sigmoid(hg)) * hu * gate).astype(BF16), wd[0])
        y_ref[...] = y

    @pl.when(i >= nused_ref[0])
    def _():
        y_ref[...] = jnp.zeros_like(y_ref)


def _moe_ln_sparse(x, gates, counts, xr, wg, wu, wd, g, b):
    n, d = x.shape
    tm = MOE_TM
    f = wg.shape[2]
    record = gates[:, ROUTE_LANE0 + 4:ROUTE_LANE0 + 6].astype(jnp.int32)
    pos, last_tile_row, tile_lo, tile_hi, n_used = _route_plan(
        record[:, 0], record[:, 1], counts[0, :N_CLASSES].astype(jnp.int32), n, tm)
    n_tiles = (n + N_CLASSES * tm) // tm
    n_rows = n_tiles * tm
    two_sems = pltpu.SemaphoreType.DMA((2,))
    xs = pl.pallas_call(
        functools.partial(_dispatch_rows_kernel, tm=tm, n_tiles=n_tiles),
        grid_spec=pltpu.PrefetchScalarGridSpec(
            num_scalar_prefetch=3, grid=(n // tm,),
            in_specs=[pl.BlockSpec((tm, d + LANES), lambda i, ps, la, u: (i, 0))],
            out_specs=pl.BlockSpec(memory_space=pl.ANY),
            scratch_shapes=[pltpu.VMEM((tm, d + LANES), F32), two_sems]),
        out_shape=jax.ShapeDtypeStruct((n_rows, d + LANES), F32),
        compiler_params=_cparams(("arbitrary",)),
        name="moe_dispatch",
    )(pos, last_tile_row, n_used, xr)
    wspec = lambda which, shape: pl.BlockSpec(
        (1,) + shape, (lambda i, lo, hi, u: (lo[i], 0, 0)) if which == 0 else (lambda i, lo, hi, u: (hi[i], 0, 0)))
    ys = pl.pallas_call(
        functools.partial(_expert_pair_kernel, d=d),
        grid_spec=pltpu.PrefetchScalarGridSpec(
            num_scalar_prefetch=3, grid=(n_tiles,),
            in_specs=[pl.BlockSpec((tm, d + LANES), lambda i, lo, hi, u: (i, 0)),
                      wspec(0, (d, f)), wspec(0, (d, f)), wspec(0, (f, d)),
                      wspec(1, (d, f)), wspec(1, (d, f)), wspec(1, (f, d))],
            out_specs=pl.BlockSpec((tm, d), lambda i, lo, hi, u: (i, 0))),
        out_shape=jax.ShapeDtypeStruct((n_rows, d), F32),
        compiler_params=_cparams(("arbitrary",)),
        name="moe_experts",
    )(tile_lo, tile_hi, n_used, xs, wg, wu, wd, wg, wu, wd)
    return pl.pallas_call(
        functools.partial(_combine_ln_kernel, tm=tm),
        grid_spec=pltpu.PrefetchScalarGridSpec(
            num_scalar_prefetch=1, grid=(n // tm,),
            in_specs=[pl.BlockSpec(memory_space=pl.ANY),
                      pl.BlockSpec((tm, d), lambda i, ps: (i, 0)),
                      pl.BlockSpec((1, d), lambda i, ps: (0, 0)),
                      pl.BlockSpec((1, d), lambda i, ps: (0, 0))],
            out_specs=pl.BlockSpec((tm, d), lambda i, ps: (i, 0)),
            scratch_shapes=[pltpu.VMEM((tm, d), F32), two_sems]),
        out_shape=jax.ShapeDtypeStruct((n, d), F32),
        compiler_params=_cparams(("arbitrary",)),
        name="moe_combine_ln",
    )(pos, ys, x, g.reshape(1, d), b.reshape(1, d))


EVEN_WIDTHS = (SB_WIDTH, SB_WIDTH, SB_WIDTH, 3 * GDN_WIDTH, GDN_WIDTH, LANES)
ODD_WIDTHS = (ML_QK_WIDTH, ML_QK_WIDTH, ML_V_WIDTH, ML_V_WIDTH, LANES)
ROW_TILE = 512


def _pad_cols(w, total):
    return jnp.pad(w, ((0, 0), (0, total - w.shape[1])))


def _pad_tokens(a, t_pad):
    return jnp.pad(a, ((0, 0), (0, t_pad - a.shape[1])) + ((0, 0),) * (a.ndim - 2))


def _run_trunk(x, past, conv_buf, gdn_state, ml_state, p):
    b, t, d = x.shape
    n = b * t
    t_pad = max(t, CHUNK)
    xt = x.reshape(n, d)

    q_sb, k_sb, v_sb, qkv_g, z_g, ab = _project(xt, p['even_w_in'], EVEN_WIDTHS, ROW_TILE)
    q3, k3, v3 = (a.reshape(b, t, SB_WIDTH) for a in (q_sb, k_sb, v_sb))
    if past is None:
        o_sb = _sb_prompt(q3, k3, v3, p['sb_bias'])
    else:
        o_sb = _sb_sample(q3, k3, v3, past[0], past[1], past[2], p['sb_bias'])
    qkv3 = qkv_g.reshape(b, t, 3 * GDN_WIDTH)
    conv_prev8 = jnp.pad(conv_buf, ((0, 0), (8 - (GDN_CONV - 1), 0), (0, 0)))
    o_g, s_out = _gdn(_pad_tokens(qkv3, t_pad), _pad_tokens(z_g.reshape(b, t, GDN_WIDTH), t_pad),
                      _pad_tokens(ab.reshape(b, t, LANES), t_pad), conv_prev8, gdn_state,
                      p['gdn_conv_w'], p['gdn_a_log'], p['gdn_dt_bias'], p['gdn_norm_w'], t)
    new_buf = jnp.concatenate([conv_buf, qkv3], axis=1)[:, -(GDN_CONV - 1):]
    merged = jnp.concatenate([o_sb, o_g[:, :t]], axis=-1).reshape(n, SB_WIDTH + GDN_WIDTH)
    xt = _outproj_ln(merged, p['even_w_out'], xt, p['ln_mix_g'][0], p['ln_mix_b'][0], ROW_TILE)
    xt = _ffn(xt, p, 0)

    q_m, k_m, v_m, o_m, ifg = _project(xt, p['odd_w_in'], ODD_WIDTHS, ROW_TILE)
    c0, n0, m0 = ml_state
    h_m, c_out, n_out, m_out = _mlstm(
        _pad_tokens(q_m.reshape(b, t, ML_QK_WIDTH), t_pad), _pad_tokens(k_m.reshape(b, t, ML_QK_WIDTH), t_pad),
        _pad_tokens(v_m.reshape(b, t, ML_V_WIDTH), t_pad), _pad_tokens(o_m.reshape(b, t, ML_V_WIDTH), t_pad),
        _pad_tokens(ifg.reshape(b, t, LANES), t_pad), c0, n0, m0,
        p['mlstm_i_bias'], p['mlstm_f_bias'], p['mlstm_norm_w'], t)
    xt = _outproj_ln(h_m[:, :t].reshape(n, ML_V_WIDTH), p['odd_w_out'], xt,
                     p['ln_mix_g'][1], p['ln_mix_b'][1], ROW_TILE)
    xt = _ffn(xt, p, 1)

    kv_shape = (1, b, t, SB_HEADS, SB_HEAD_DIM)
    return (xt.reshape(b, t, d), k_sb.reshape(kv_shape), v_sb.reshape(kv_shape), new_buf[None], s_out[None],
            c_out[None], n_out[None], m_out[None])


def _ffn(xt, p, layer):
    sparse = xt.shape[0] >= SPARSE_MIN_TOKENS
    routed = _router(xt, p['moe_w_group'][layer], p['moe_b_group'][layer], p['moe_w_fine'][layer],
                     p['moe_b_fine'][layer], ROW_TILE, sparse)
    weights = (p['moe_w_gate'][layer], p['moe_w_up'][layer], p['moe_w_down'][layer],
               p['ln_ffn_g'][layer], p['ln_ffn_b'][layer])
    if sparse:
        return _moe_ln_sparse(xt, routed[0], routed[1], routed[2], *weights)
    return _moe_ln(xt, routed[0], *weights, ROW_TILE)


def kernel(x_prompt, x_sample, cache_k, cache_v, state_conv, state_gdn, state_mlstm_C, state_mlstm_n,
           state_mlstm_m, page_table, even_w_in, even_w_out, sb_bias, gdn_conv_w, gdn_a_log, gdn_dt_bias,
           gdn_norm_w, odd_w_in, odd_w_out, mlstm_i_bias, mlstm_f_bias, mlstm_norm_w, ln_mix_g, ln_mix_b,
           ln_ffn_g, ln_ffn_b, moe_w_group, moe_b_group, moe_w_fine, moe_b_fine, moe_w_gate, moe_w_up,
           moe_w_down):
    assert DEPTH == 2 and even_w_in.shape[0] == 1 and odd_w_in.shape[0] == 1
    p = {
        'even_w_in': _pad_cols(even_w_in[0], sum(EVEN_WIDTHS)).astype(BF16),
        'even_w_out': even_w_out[0].astype(BF16),
        'sb_bias': sb_bias[0], 'gdn_conv_w': gdn_conv_w[0], 'gdn_a_log': gdn_a_log[0],
        'gdn_dt_bias': gdn_dt_bias[0], 'gdn_norm_w': gdn_norm_w[0],
        'odd_w_in': _pad_cols(odd_w_in[0], sum(ODD_WIDTHS)).astype(BF16),
        'odd_w_out': odd_w_out[0].astype(BF16),
        'mlstm_i_bias': mlstm_i_bias[0], 'mlstm_f_bias': mlstm_f_bias[0], 'mlstm_norm_w': mlstm_norm_w[0],
        'ln_mix_g': ln_mix_g, 'ln_mix_b': ln_mix_b, 'ln_ffn_g': ln_ffn_g, 'ln_ffn_b': ln_ffn_b,
        'moe_w_group': moe_w_group, 'moe_b_group': moe_b_group, 'moe_w_fine': moe_w_fine,
        'moe_b_fine': moe_b_fine, 'moe_w_gate': moe_w_gate.astype(BF16), 'moe_w_up': moe_w_up.astype(BF16),
        'moe_w_down': moe_w_down.astype(BF16),
    }
    bp = x_prompt.shape[0]
    zero_buf = jnp.zeros((bp, GDN_CONV - 1, 3 * GDN_WIDTH), F32)
    zero_s = jnp.zeros((bp, GDN_HEADS, GDN_HEAD_DIM, GDN_HEAD_DIM), F32)
    zero_ml = (jnp.zeros((bp, ML_HEADS, ML_QK_DIM, ML_V_DIM), F32), jnp.zeros((bp, ML_HEADS, ML_QK_DIM), F32),
               jnp.zeros((bp, ML_HEADS), F32))
    (y_p, k_p, v_p, conv_p, gdn_p, mc_p, mn_p, mm_p) = _run_trunk(x_prompt, None, zero_buf, zero_s, zero_ml, p)

    pool_k = jnp.transpose(cache_k.reshape(cache_k.shape[1:]), (0, 2, 3, 1))
    pool_v = jnp.transpose(cache_v.reshape(cache_v.shape[1:]), (0, 2, 3, 1))
    (y_s, k_s, v_s, conv_s, gdn_s, mc_s, mn_s, mm_s) = _run_trunk(
        x_sample, (pool_k, pool_v, page_table), state_conv[0], state_gdn[0],
        (state_mlstm_C[0], state_mlstm_n[0], state_mlstm_m[0]), p)
    return (y_p, y_s, k_p, v_p, k_s, v_s, conv_p, conv_s, gdn_p, gdn_s, mc_p, mc_s, mn_p, mn_s, mm_p, mm_s)
```

```python
import functools
import math

import jax
import jax.numpy as jnp
from jax import lax
from jax.experimental import pallas as pl
from jax.experimental.pallas import tpu as pltpu

F32 = jnp.float32
BF16 = jnp.bfloat16

D_MODEL = 1024
DEPTH = 2
PAGE_SIZE = 128
SB_HEADS = 8
SB_HEAD_DIM = 64
SB_WIDTH = SB_HEADS * SB_HEAD_DIM
SB_SCALE = SB_HEAD_DIM ** -0.5
GDN_HEADS = 4
GDN_HEAD_DIM = 128
GDN_WIDTH = GDN_HEADS * GDN_HEAD_DIM
GDN_CONV = 4
ML_HEADS = 8
ML_QK_DIM = 64
ML_V_DIM = 128
ML_QK_WIDTH = ML_HEADS * ML_QK_DIM
ML_V_WIDTH = ML_HEADS * ML_V_DIM
N_GROUPS = 4
EXPERTS_PER_GROUP = 4
N_EXPERTS = N_GROUPS * EXPERTS_PER_GROUP
EXPERT_FF = 512
DN_ALPHA = (2 * DEPTH) ** 0.25
LN_EPS = 1e-5
RMS_EPS = 1e-6
L2_EPS = 1e-6

LANES = 128
MAX_CHUNK = 64
MIN_CHUNK = 16
NEG_BIG = -1e30
VMEM_LIMIT = 56 * 1024 * 1024


def _cparams(sem):
    return pltpu.CompilerParams(dimension_semantics=sem, vmem_limit_bytes=VMEM_LIMIT)


def _mm(a, b):
    return jnp.dot(a, b, preferred_element_type=F32)


def _mm_nt(a, b):
    return lax.dot_general(a, b, (((1,), (1,)), ((), ())), preferred_element_type=F32)


def _mm_tn(a, b):
    return lax.dot_general(a, b, (((0,), (0,)), ((), ())), preferred_element_type=F32)


def _split3(x):
    x1 = x.astype(BF16)
    r1 = x - x1.astype(F32)
    x2 = r1.astype(BF16)
    x3 = (r1 - x2.astype(F32)).astype(BF16)
    return x1, x2, x3


def _mm_exact_lhs(a_bf16, b):
    b1, b2, b3 = _split3(b)
    return _mm(a_bf16, b1) + _mm(a_bf16, b2) + _mm(a_bf16, b3)


def _mm3(a, b):
    a1, a2, _ = _split3(a)
    b1, b2, _ = _split3(b)
    return _mm(a1, b1) + (_mm(a1, b2) + _mm(a2, b1))


def _softplus(x):
    return jnp.maximum(x, 0.0) + jnp.log1p(jnp.exp(-jnp.abs(x)))


def _sigmoid(x):
    return jax.nn.sigmoid(x)


def _iota(shape, dim):
    return lax.broadcasted_iota(jnp.int32, shape, dim)


def _proj_kernel(x_ref, w_ref, *out_refs, widths):
    x = x_ref[...].astype(BF16)
    off = 0
    for o_ref, wd in zip(out_refs, widths):
        o_ref[...] = _mm(x, w_ref[:, off:off + wd])
        off += wd


def _project(x, w_bf16, widths, tm):
    n, d = x.shape
    tm = min(tm, n)
    total = sum(widths)
    return pl.pallas_call(
        functools.partial(_proj_kernel, widths=tuple(widths)),
        grid=(n // tm,),
        in_specs=[pl.BlockSpec((tm, d), lambda i: (i, 0)),
                  pl.BlockSpec((d, total), lambda i: (0, 0))],
        out_specs=[pl.BlockSpec((tm, wd), lambda i: (i, 0)) for wd in widths],
        out_shape=[jax.ShapeDtypeStruct((n, wd), F32) for wd in widths],
        compiler_params=_cparams(("parallel",)),
        name="proj",
    )(x, w_bf16)


def _layer_norm_rows(r, g, b):
    mu = jnp.mean(r, axis=-1, keepdims=True)
    c = r - mu
    var = jnp.mean(c * c, axis=-1, keepdims=True)
    return c * lax.rsqrt(var + LN_EPS) * g + b


def _outproj_ln_kernel(m_ref, w_ref, x_ref, g_ref, b_ref, o_ref):
    y = _mm(m_ref[...].astype(BF16), w_ref[...])
    o_ref[...] = _layer_norm_rows(DN_ALPHA * x_ref[...] + y, g_ref[...], b_ref[...])


def _outproj_ln(merged, w_bf16, x, g, b, tm):
    n, k = merged.shape
    d = x.shape[1]
    tm = min(tm, n)
    return pl.pallas_call(
        _outproj_ln_kernel,
        grid=(n // tm,),
        in_specs=[pl.BlockSpec((tm, k), lambda i: (i, 0)),
                  pl.BlockSpec((k, d), lambda i: (0, 0)),
                  pl.BlockSpec((tm, d), lambda i: (i, 0)),
                  pl.BlockSpec((1, d), lambda i: (0, 0)),
                  pl.BlockSpec((1, d), lambda i: (0, 0))],
        out_specs=pl.BlockSpec((tm, d), lambda i: (i, 0)),
        out_shape=jax.ShapeDtypeStruct((n, d), F32),
        compiler_params=_cparams(("parallel",)),
        name="outproj_ln",
    )(merged, w_bf16, x, g.reshape(1, d), b.reshape(1, d))


SB_SUB = 256
SB_TQ = 512
SB_TKB = 1024


LOG2E = math.log2(math.e)


def _suffix_sum_matrix(n):
    return (jnp.arange(n)[:, None] > jnp.arange(n)[None, :]).astype(BF16)


def _sb_chains(chains, u):
    logs = [[jnp.minimum(nz, 0.0) - jnp.log(1.0 + jnp.exp2(jnp.abs(nz) * -LOG2E)) for nz in tiles]
            for tiles, _, _ in chains]
    masked = [[l if c is None else jnp.where(c, l, 0.0) for l, c in zip(ls, causals)]
              for ls, (_, causals, _) in zip(logs, chains)]
    sums = [[_mm(lm.astype(BF16), u) for lm in lms] for lms in masked]
    out = []
    for ls, lms, ss, (tiles, causals, r_b) in zip(logs, masked, sums, chains):
        ws = []
        for nz, l, lm, s, c in zip(tiles, ls, lms, ss, causals):
            tk = nz.shape[1]
            tail = s + jnp.concatenate([r_b] * (tk // LANES), axis=1)
            w = jnp.exp2(((l - nz) + tail) * LOG2E)
            ws.append(w if c is None else jnp.where(c, w, 0.0))
            r_b = r_b + jnp.broadcast_to(s[:, :1] + lm[:, :1], r_b.shape)
        out.append((ws, r_b))
    return out


def _sb_prompt_kernel(qi_ref, kb_ref, bias_ref, u_ref, q_ref, k_ref, v_ref, o_ref, acc_ref, r_ref,
                      *, tq, tkb, sub):
    hp = pl.program_id(1)
    p = pl.program_id(2)
    q0 = qi_ref[p] * tq
    k0 = kb_ref[p] * tkb

    @pl.when(k0 + tkb >= q0 + tq)
    def _():
        acc_ref[...] = jnp.zeros_like(acc_ref)
        r_ref[...] = jnp.zeros_like(r_ref)

    def body(masked):
        lane = _iota((1, LANES), 1)
        nq2 = q_ref[0] * (-SB_SCALE)
        u = u_ref[...]
        ones_blk = jnp.where(_iota((sub, LANES), 1) < 3, 1.0, 0.0).astype(BF16)
        subs = range(tkb // sub - 1, -1, -1)
        k_ext = [jnp.concatenate([k_ref[0, s * sub:(s + 1) * sub, :].astype(BF16), ones_blk], axis=1) for s in subs]
        causals = [None] * len(subs)
        if masked:
            causals = [(_iota((tq, sub), 1) + (k0 + s * sub)) < (_iota((tq, sub), 0) + q0) for s in subs]
        chains, in_heads = [], []
        for e in range(2):
            in_head = (lane >= SB_HEAD_DIM * e) & (lane < SB_HEAD_DIM * (e + 1))
            nqh = jnp.where(in_head, nq2, 0.0).astype(BF16)
            nb = [part.astype(F32) for part in _split3(jnp.full((1, LANES), -bias_ref[2 * hp + e], F32))]
            bias_row = jnp.where(lane == 0, nb[0], jnp.where(lane == 1, nb[1], jnp.where(lane == 2, nb[2], 0.0)))
            nq_ext = jnp.concatenate([nqh, jnp.broadcast_to(bias_row, (tq, LANES)).astype(BF16)], axis=1)
            chains.append(([_mm_nt(nq_ext, ks) for ks in k_ext], causals, r_ref[e]))
            in_heads.append(in_head)
        upd = jnp.zeros((tq, LANES), F32)
        for e, (ws, r_b) in enumerate(_sb_chains(chains, u)):
            r_ref[e] = r_b
            for w, s in zip(ws, subs):
                vs = jnp.where(in_heads[e], v_ref[0, s * sub:(s + 1) * sub, :], 0.0).astype(BF16)
                upd = upd + _mm(w.astype(BF16), vs)
        acc_ref[...] += upd

    needs_mask = k0 + tkb > q0

    @pl.when(needs_mask)
    def _():
        body(True)

    @pl.when(jnp.logical_not(needs_mask))
    def _():
        body(False)

    @pl.when(k0 == 0)
    def _():
        o_ref[0] = acc_ref[...]


def _sb_prompt(q, k, v, bias):
    b, t, _ = q.shape
    tq, tkb = min(SB_TQ, t), min(SB_TKB, t)
    sub = min(SB_SUB, tkb)
    qi_list, kb_list = [], []
    for i in range(t // tq):
        for kb in range(-(-(i + 1) * tq // tkb) - 1, -1, -1):
            qi_list.append(i)
            kb_list.append(kb)
    qi = jnp.asarray(qi_list, jnp.int32)
    kb = jnp.asarray(kb_list, jnp.int32)
    grid_spec = pltpu.PrefetchScalarGridSpec(
        num_scalar_prefetch=2,
        grid=(b, SB_HEADS // 2, len(qi_list)),
        in_specs=[pl.BlockSpec(memory_space=pltpu.SMEM),
                  pl.BlockSpec((sub, sub), lambda bb, hp, p, qi, kb: (0, 0)),
                  pl.BlockSpec((1, tq, LANES), lambda bb, hp, p, qi, kb: (bb, qi[p], hp)),
                  pl.BlockSpec((1, tkb, LANES), lambda bb, hp, p, qi, kb: (bb, kb[p], hp)),
                  pl.BlockSpec((1, tkb, LANES), lambda bb, hp, p, qi, kb: (bb, kb[p], hp))],
        out_specs=pl.BlockSpec((1, tq, LANES), lambda bb, hp, p, qi, kb: (bb, qi[p], hp)),
        scratch_shapes=[pltpu.VMEM((tq, LANES), F32), pltpu.VMEM((2, tq, LANES), F32)],
    )
    return pl.pallas_call(
        functools.partial(_sb_prompt_kernel, tq=tq, tkb=tkb, sub=sub),
        grid_spec=grid_spec,
        out_shape=jax.ShapeDtypeStruct((b, t, SB_WIDTH), F32),
        compiler_params=_cparams(("parallel", "parallel", "arbitrary")),
        name="sb_prompt",
    )(qi, kb, bias, _suffix_sum_matrix(sub), q, k, v)


PAGES_PER_STEP = 16


def _sb_sample_kernel(pt_ref, bias_ref, u_ref, q_ref, kn_ref, vn_ref, *rest, t_new):
    k_refs = rest[:PAGES_PER_STEP]
    v_refs = rest[PAGES_PER_STEP:2 * PAGES_PER_STEP]
    o_ref, nq_ref, nb_ref, acc_ref, r_ref = rest[2 * PAGES_PER_STEP:]
    rows = SB_HEADS * t_new
    j = pl.program_id(1)

    def process(page_k_refs, page_v_refs, masked):
        n_pg = len(page_k_refs)
        keys = lambda refs, h: jnp.concatenate([ref[0, h].astype(BF16) for ref in refs], axis=1)
        nz = jnp.concatenate([_mm(nq_ref[h], keys(page_k_refs, h)) for h in range(SB_HEADS)], axis=0)
        nz = nz + jnp.concatenate([nb_ref[...]] * n_pg, axis=1)
        causal = None
        if masked:
            causal = _iota((rows, PAGE_SIZE), 1) < (_iota((rows, PAGE_SIZE), 0) % t_new)
        tiles = [nz[:, g * PAGE_SIZE:(g + 1) * PAGE_SIZE] for g in range(n_pg)]
        (ws, r_b), = _sb_chains([(tiles, [causal] * n_pg, r_ref[...])], u_ref[...])
        r_ref[...] = r_b
        w = jnp.concatenate(ws, axis=1)
        for h in range(SB_HEADS):
            acc_ref[h] += _mm_nt(w[h * t_new:(h + 1) * t_new].astype(BF16), keys(page_v_refs, h))

    @pl.when(j == 0)
    def _():
        nq_ref[...] = (q_ref[0] * (-SB_SCALE)).astype(BF16)
        hrow = _iota((rows, LANES), 0) // t_new
        nb = jnp.zeros((rows, LANES), F32)
        for h in range(SB_HEADS):
            nb = jnp.where(hrow == h, -bias_ref[h], nb)
        nb_ref[...] = nb
        acc_ref[...] = jnp.zeros_like(acc_ref)
        r_ref[...] = jnp.zeros_like(r_ref)
        process([kn_ref], [vn_ref], True)

    process(k_refs, v_refs, False)

    @pl.when(j == pl.num_programs(1) - 1)
    def _():
        o_ref[0] = acc_ref[...]


def _sb_sample(q, k_new, v_new, pool_k, pool_v, page_table, bias):
    b, t_new, _ = q.shape
    n_pages = page_table.shape[1]
    steps = n_pages // PAGES_PER_STEP
    rows = SB_HEADS * t_new
    page_block = (1, SB_HEADS, SB_HEAD_DIM, PAGE_SIZE)

    def as_page(a):
        a = a.reshape(b, t_new, SB_HEADS, SB_HEAD_DIM).transpose(0, 2, 3, 1)
        return jnp.pad(a, ((0, 0), (0, 0), (0, 0), (0, PAGE_SIZE - t_new)))

    q_heads = q.reshape(b, t_new, SB_HEADS, SB_HEAD_DIM).transpose(0, 2, 1, 3)

    def page_spec(r):
        def imap(bb, j, pt):
            return (pt[bb, n_pages - 1 - (j * PAGES_PER_STEP + r)], 0, 0, 0)
        return pl.BlockSpec(page_block, imap)

    new_spec = pl.BlockSpec(page_block, lambda bb, j, pt: (bb, 0, 0, 0))
    qo_spec = pl.BlockSpec((1, SB_HEADS, t_new, SB_HEAD_DIM), lambda bb, j, pt: (bb, 0, 0, 0))
    grid_spec = pltpu.PrefetchScalarGridSpec(
        num_scalar_prefetch=1,
        grid=(b, steps),
        in_specs=([pl.BlockSpec(memory_space=pltpu.SMEM),
                   pl.BlockSpec((PAGE_SIZE, PAGE_SIZE), lambda bb, j, pt: (0, 0)),
                   qo_spec, new_spec, new_spec]
                  + [page_spec(r) for r in range(PAGES_PER_STEP)]
                  + [page_spec(r) for r in range(PAGES_PER_STEP)]),
        out_specs=qo_spec,
        scratch_shapes=[pltpu.VMEM((SB_HEADS, t_new, SB_HEAD_DIM), BF16), pltpu.VMEM((rows, LANES), F32),
                        pltpu.VMEM((SB_HEADS, t_new, SB_HEAD_DIM), F32), pltpu.VMEM((rows, LANES), F32)],
    )
    o = pl.pallas_call(
        functools.partial(_sb_sample_kernel, t_new=t_new),
        grid_spec=grid_spec,
        out_shape=jax.ShapeDtypeStruct((b, SB_HEADS, t_new, SB_HEAD_DIM), F32),
        compiler_params=_cparams(("parallel", "arbitrary")),
        name="sb_sample",
    )(page_table, bias, _suffix_sum_matrix(PAGE_SIZE), q_heads, as_page(k_new), as_page(v_new),
      *([pool_k] * PAGES_PER_STEP), *([pool_v] * PAGES_PER_STEP))
    return o.transpose(0, 2, 1, 3).reshape(b, t_new, SB_WIDTH)


def _chunk_masks(ch):
    i = _iota((ch, ch), 0)
    j = _iota((ch, ch), 1)
    return i >= j, i > j


def _gdn_prep_kernel(alog_ref, dtb_ref, qr_ref, kr_ref, vr_ref, qp_ref, kp_ref, vp_ref,
                     qs_ref, ks_ref, vs_ref, wq_ref, wk_ref, wv_ref, ab_ref,
                     u_ref, w_ref, qg_ref, kd_ref, sd_ref, eg_ref, xbuf_ref, *, tc, ch, t_valid):
    h = pl.program_id(1)
    c = pl.program_id(2)

    def conv_silu(x_ref, prev_ref, state_ref, wt_ref):
        first = (c == 0).astype(F32)
        xbuf_ref[0:8, :] = first * state_ref[0] + (1.0 - first) * prev_ref[0]
        xbuf_ref[8:, :] = x_ref[0]
        y = jnp.zeros((tc, LANES), F32)
        for i in range(GDN_CONV):
            y = y + xbuf_ref[5 + i:5 + i + tc, :] * wt_ref[i:i + 1, :]
        return y * _sigmoid(y)

    qc = conv_silu(qr_ref, qp_ref, qs_ref, wq_ref)
    kc = conv_silu(kr_ref, kp_ref, ks_ref, wk_ref)
    vc = conv_silu(vr_ref, vp_ref, vs_ref, wv_ref)
    q = qc * lax.rsqrt(jnp.sum(qc * qc, axis=-1, keepdims=True) + L2_EPS) * GDN_HEAD_DIM ** -0.5
    k = kc * lax.rsqrt(jnp.sum(kc * kc, axis=-1, keepdims=True) + L2_EPS)

    ab = ab_ref[0]
    lane = _iota((tc, LANES), 1)
    a_b = jnp.broadcast_to(jnp.sum(jnp.where(lane == h, ab, 0.0), axis=-1, keepdims=True), (tc, LANES))
    b_b = jnp.broadcast_to(jnp.sum(jnp.where(lane == GDN_HEADS + h, ab, 0.0), axis=-1, keepdims=True),
                           (tc, LANES))
    neg_a = -jnp.exp(jnp.full((1, LANES), alog_ref[h], F32))
    g_b = neg_a * _softplus(a_b + dtb_ref[h])
    beta_b = _sigmoid(b_b)
    valid = (_iota((tc, LANES), 0) + c * tc) < t_valid
    g_b = jnp.where(valid, g_b, 0.0)
    beta_b = jnp.where(valid, beta_b, 0.0)

    lower, strict = _chunk_masks(ch)
    t1 = lower.astype(BF16)
    t2 = (_iota((ch, ch), 0) > _iota((ch, ch), 1)).astype(F32)
    eye = (_iota((ch, ch), 0) == _iota((ch, ch), 1)).astype(F32)

    chunks = range(tc // ch)
    sls = [slice(n * ch, (n + 1) * ch) for n in chunks]
    gc_b = [_mm_exact_lhs(t1, g_b[sl]) for sl in sls]
    dg = [_mm_exact_lhs(t1, g_b[sl][:, :ch] * t2) for sl in sls]
    decay = [jnp.where(lower, jnp.exp(jnp.where(lower, d, 0.0)), 0.0) for d in dg]
    kbeta = [k[sl] * beta_b[sl] for sl in sls]
    k16 = [k[sl].astype(BF16) for sl in sls]
    a = [jnp.where(strict, _mm_nt(kbeta[n].astype(BF16), k16[n]) * decay[n], 0.0) for n in chunks]
    tinv = [eye - a[n] for n in chunks]
    pw = a
    for _ in range(int(math.log2(ch)) - 1):
        pw = [_mm3(pw[n], pw[n]) for n in chunks]
        tinv = [tinv[n] + _mm3(tinv[n], pw[n]) for n in chunks]
    sol = [_mm3(tinv[n], jnp.concatenate([vc[sls[n]] * beta_b[sls[n]], kbeta[n] * jnp.exp(gc_b[n])], axis=1))
           for n in chunks]
    for n in chunks:
        sl = sls[n]
        u_ref[0, 0, sl, :] = sol[n][:, :GDN_HEAD_DIM]
        w_ref[0, 0, sl, :] = sol[n][:, GDN_HEAD_DIM:].astype(BF16)
        sd_ref[0, 0, sl, :] = (_mm_nt(q[sl].astype(BF16), k16[n]) * decay[n]).astype(BF16)
        qg_ref[0, 0, sl, :] = (q[sl] * jnp.exp(gc_b[n])).astype(BF16)
        g_last = gc_b[n][ch - 1:ch, :]
        kd_ref[0, 0, sl, :] = (k[sl] * jnp.exp(g_last - gc_b[n])).astype(BF16)
        eg_ref[0, 0, n:n + 1, :] = jnp.exp(g_last)


def _gdn_scan_kernel(u_ref, w_ref, qg_ref, kd_ref, sd_ref, eg_ref, z_ref, nw_ref, s0_ref,
                     o_ref, sout_ref, s_ref, *, tc, ch):
    c = pl.program_id(1)
    hd = GDN_HEAD_DIM

    @pl.when(c == 0)
    def _():
        s_ref[...] = s0_ref[0]

    s = [s_ref[h] for h in range(GDN_HEADS)]
    for n in range(tc // ch):
        sl = slice(n * ch, (n + 1) * ch)
        for h in range(GDN_HEADS):
            s16 = s[h].astype(BF16)
            v_new = u_ref[0, h, sl, :] - _mm(w_ref[0, h, sl, :], s16)
            vn16 = v_new.astype(BF16)
            o = _mm(qg_ref[0, h, sl, :], s16) + _mm(sd_ref[0, h, sl, :], vn16)
            s[h] = eg_ref[0, h, n:n + 1, :] * s[h] + _mm_tn(kd_ref[0, h, sl, :], vn16)
            zg = z_ref[0, sl, h * hd:(h + 1) * hd]
            on = o * lax.rsqrt(jnp.mean(o * o, axis=-1, keepdims=True) + RMS_EPS) * nw_ref[...]
            o_ref[0, sl, h * hd:(h + 1) * hd] = on * (zg * _sigmoid(zg))
    for h in range(GDN_HEADS):
        s_ref[h] = s[h]

    @pl.when(c == pl.num_programs(1) - 1)
    def _():
        for h in range(GDN_HEADS):
            sout_ref[0, h] = s[h]


def _gdn(qkv, z, ab, conv_prev8, s0, conv_w, a_log, dt_bias, norm_w, t_valid):
    b, t, _ = qkv.shape
    tc = min(512, t)
    ch = min(MAX_CHUNK, t)
    nb = t // tc
    nch = t // ch
    hd = GDN_HEAD_DIM
    col = lambda off: pl.BlockSpec((1, tc, hd), lambda bb, h, c: (bb, c, h + off))
    prev = lambda off: pl.BlockSpec((1, 8, hd), lambda bb, h, c: (bb, jnp.maximum(c * (tc // 8) - 1, 0), h + off))
    state = lambda off: pl.BlockSpec((1, 8, hd), lambda bb, h, c: (bb, 0, h + off))
    wspec = lambda off: pl.BlockSpec((GDN_CONV, hd), lambda bb, h, c: (0, h + off))
    per_head = lambda width: pl.BlockSpec((1, 1, tc, width), lambda bb, h, c: (bb, h, c, 0))
    eg_spec = pl.BlockSpec((1, 1, tc // ch, hd), lambda bb, h, c: (bb, h, c, 0))
    smem = pl.BlockSpec(memory_space=pltpu.SMEM)
    offs = (0, GDN_HEADS, 2 * GDN_HEADS)
    u, w, qg, kd, sd, eg = pl.pallas_call(
        functools.partial(_gdn_prep_kernel, tc=tc, ch=ch, t_valid=t_valid),
        grid=(b, GDN_HEADS, nb),
        in_specs=([smem, smem] + [col(o) for o in offs] + [prev(o) for o in offs]
                  + [state(o) for o in offs] + [wspec(o) for o in offs]
                  + [pl.BlockSpec((1, tc, LANES), lambda bb, h, c: (bb, c, 0))]),
        out_specs=[per_head(hd), per_head(hd), per_head(hd), per_head(hd), per_head(ch), eg_spec],
        out_shape=[jax.ShapeDtypeStruct((b, GDN_HEADS, t, hd), F32)]
                  + [jax.ShapeDtypeStruct((b, GDN_HEADS, t, hd), BF16)] * 3
                  + [jax.ShapeDtypeStruct((b, GDN_HEADS, t, ch), BF16),
                     jax.ShapeDtypeStruct((b, GDN_HEADS, nch, hd), F32)],
        scratch_shapes=[pltpu.VMEM((tc + 8, hd), F32)],
        compiler_params=_cparams(("parallel", "parallel", "parallel")),
        name="gdn_prep",
    )(a_log, dt_bias, qkv, qkv, qkv, qkv, qkv, qkv, conv_prev8, conv_prev8, conv_prev8,
      conv_w, conv_w, conv_w, ab)
    all_heads = lambda width: pl.BlockSpec((1, GDN_HEADS, tc, width), lambda bb, c: (bb, 0, c, 0))
    state_spec = pl.BlockSpec((1, GDN_HEADS, hd, hd), lambda bb, c: (bb, 0, 0, 0))
    og, s_out = pl.pallas_call(
        functools.partial(_gdn_scan_kernel, tc=tc, ch=ch),
        grid=(b, nb),
        in_specs=[all_heads(hd), all_heads(hd), all_heads(hd), all_heads(hd), all_heads(ch),
                  pl.BlockSpec((1, GDN_HEADS, tc // ch, hd), lambda bb, c: (bb, 0, c, 0)),
                  pl.BlockSpec((1, tc, GDN_WIDTH), lambda bb, c: (bb, c, 0)),
                  pl.BlockSpec((1, hd), lambda bb, c: (0, 0)),
                  state_spec],
        out_specs=[pl.BlockSpec((1, tc, GDN_WIDTH), lambda bb, c: (bb, c, 0)), state_spec],
        out_shape=[jax.ShapeDtypeStruct((b, t, GDN_WIDTH), F32),
                   jax.ShapeDtypeStruct((b, GDN_HEADS, hd, hd), F32)],
        scratch_shapes=[pltpu.VMEM((GDN_HEADS, hd, hd), F32)],
        compiler_params=_cparams(("parallel", "arbitrary")),
        name="gdn_scan",
    )(u, w, qg, kd, sd, eg, z, norm_w.reshape(1, hd), s0)
    return og, s_out


def _mlstm_kernel(ib_ref, fb_ref, m0_ref, q_ref, k_ref, v_ref, og_ref, if_ref, c0_ref, n0_ref, nw_ref,
                  h_ref, cout_ref, nout_ref, mout_ref, c_ref, n_ref, m_ref, *, tc, ch, t_valid):
    bb = pl.program_id(0)
    p = pl.program_id(1)
    c = pl.program_id(2)
    vd = ML_V_DIM

    @pl.when(c == 0)
    def _():
        c_ref[...] = c0_ref[0, 0]
        n_ref[...] = n0_ref[0, 0]
        for e in range(2):
            m_ref[e] = jnp.full((1, LANES), m0_ref[bb, 2 * p + e], F32)

    lower, _ = _chunk_masks(ch)
    t1 = lower.astype(BF16)
    t2 = (_iota((ch, ch), 0) > _iota((ch, ch), 1)).astype(F32)
    eye = (_iota((ch, ch), 0) == _iota((ch, ch), 1)).astype(F32)
    ones = jnp.ones((ch, ch), BF16)
    lane_row = _iota((1, LANES), 1)
    lane = _iota((tc, LANES), 1)
    gates = if_ref[0]

    def gate_col(idx):
        col = jnp.sum(jnp.where(lane == idx, gates, 0.0), axis=-1, keepdims=True)
        return jnp.broadcast_to(col, (tc, LANES))

    i_b, lf_b = [], []
    for e in range(2):
        hd = 2 * p + e
        i_b.append(gate_col(hd) + ib_ref[hd])
        lf_b.append(-_softplus(-(gate_col(ML_HEADS + hd) + fb_ref[hd])))

    chunks = range(tc // ch)
    sls = [slice(n * ch, (n + 1) * ch) for n in chunks]
    valid = [lower & ((_iota((ch, ch), 1) + (c * tc + n * ch)) < t_valid) for n in chunks]
    row_valid = [(_iota((ch, LANES), 0) + (c * tc + n * ch)) < t_valid for n in chunks]
    v16 = [[v_ref[0, sl, e * vd:(e + 1) * vd].astype(BF16) for sl in sls] for e in range(2)]
    qm, km, qm16, qk, bc_b, dm, ic = ([[None] * len(sls) for _ in range(2)] for _ in range(7))
    for e in range(2):
        in_head = (lane_row >= ML_QK_DIM * e) & (lane_row < ML_QK_DIM * (e + 1))
        for n in chunks:
            qm[e][n] = jnp.where(in_head, q_ref[0, sls[n], :], 0.0)
            km[e][n] = jnp.where(in_head, k_ref[0, sls[n], :] * ML_QK_DIM ** -0.5, 0.0)
            qm16[e][n] = qm[e][n].astype(BF16)
            qk[e][n] = _mm_nt(qm16[e][n], km[e][n].astype(BF16))
            lf = jnp.where(row_valid[n], lf_b[e][sls[n]], 0.0)
            ic[e][n] = i_b[e][sls[n]]
            bc_b[e][n] = _mm_exact_lhs(t1, lf)
            dm[e][n] = (_mm_exact_lhs(t1, lf[:, :ch] * t2)
                        + _mm_exact_lhs(ones, eye * ic[e][n][:, :ch]))

    m_t, m_in = ([[None] * len(sls) for _ in range(2)] for _ in range(2))
    for e in range(2):
        m_prev = m_ref[e]
        for n in chunks:
            m_in[e][n] = m_prev
            row_max = jnp.max(jnp.where(valid[n], dm[e][n], NEG_BIG), axis=-1, keepdims=True)
            m_t[e][n] = jnp.maximum(bc_b[e][n] + m_prev, row_max)
            m_prev = m_t[e][n][ch - 1:ch, :]
        m_ref[e] = m_prev

    s, w_state, a_state, c_upd, n_upd = ([[None] * len(sls) for _ in range(2)] for _ in range(5))
    for e in range(2):
        for n in chunks:
            mt = m_t[e][n]
            w_intra = jnp.where(valid[n], jnp.exp(jnp.where(valid[n], dm[e][n], 0.0) - mt[:, :ch]), 0.0)
            w_state[e][n] = jnp.exp(bc_b[e][n] + m_in[e][n] - mt)
            s[e][n] = qk[e][n] * w_intra
            m_new = mt[ch - 1:ch, :]
            bc_last = bc_b[e][n][ch - 1:ch, :]
            a_state[e][n] = jnp.exp(bc_last + m_in[e][n] - m_new)
            wk = jnp.where(row_valid[n], jnp.exp(bc_last - bc_b[e][n] + ic[e][n] - m_new), 0.0)
            kw = km[e][n] * wk
            c_upd[e][n] = _mm_tn(kw.astype(BF16), v16[e][n])
            n_upd[e][n] = jnp.sum(kw, axis=0, keepdims=True)

    c_state, n_state = [c_ref[...]], [n_ref[...]]
    upper_rows = _iota((LANES, vd), 0) < ML_QK_DIM
    for n in chunks:
        a_rows = jnp.where(upper_rows, a_state[0][n], a_state[1][n])
        c_state.append(a_rows * c_state[n] + (c_upd[0][n] + c_upd[1][n]))
        a_lanes = jnp.where(lane_row < ML_QK_DIM, a_state[0][n], a_state[1][n])
        n_state.append(a_lanes * n_state[n] + (n_upd[0][n] + n_upd[1][n]))
    c_ref[...] = c_state[-1]
    n_ref[...] = n_state[-1]

    for n in chunks:
        cp16 = c_state[n].astype(BF16)
        for e in range(2):
            num = w_state[e][n] * _mm(qm16[e][n], cp16) + _mm(s[e][n].astype(BF16), v16[e][n])
            qn = jnp.sum(qm[e][n] * n_state[n], axis=-1, keepdims=True)
            den = w_state[e][n] * qn + jnp.sum(s[e][n], axis=-1, keepdims=True)
            hh = num / jnp.maximum(jnp.abs(den), jnp.exp(-m_t[e][n]))
            og = og_ref[0, sls[n], e * vd:(e + 1) * vd]
            hn = hh * lax.rsqrt(jnp.mean(hh * hh, axis=-1, keepdims=True) + RMS_EPS) * nw_ref[...]
            h_ref[0, sls[n], e * vd:(e + 1) * vd] = hn * _sigmoid(og)

    @pl.when(c == pl.num_programs(2) - 1)
    def _():
        cout_ref[0, 0] = c_ref[...]
        nout_ref[0, 0] = n_ref[...]
        for e in range(2):
            mout_ref[0, e] = m_ref[e]


def _mlstm(q, k, v, og, ifg, c0, n0, m0, i_bias, f_bias, norm_w, t_valid):
    b, t, _ = q.shape
    tc = min(512, t)
    ch = min(MAX_CHUNK, t)
    nb = t // tc
    pairs = ML_HEADS // 2
    smem = pl.BlockSpec(memory_space=pltpu.SMEM)
    qk_spec = pl.BlockSpec((1, tc, LANES), lambda bb, p, c: (bb, c, p))
    v_spec = pl.BlockSpec((1, tc, 2 * ML_V_DIM), lambda bb, p, c: (bb, c, p))
    c_spec = pl.BlockSpec((1, 1, LANES, ML_V_DIM), lambda bb, p, c: (bb, p, 0, 0))
    n_spec = pl.BlockSpec((1, 1, 1, LANES), lambda bb, p, c: (bb, p, 0, 0))
    m_spec = pl.BlockSpec((1, 2, 1, LANES), lambda bb, p, c: (bb, p, 0, 0))
    h, c_out, n_out, m_out = pl.pallas_call(
        functools.partial(_mlstm_kernel, tc=tc, ch=ch, t_valid=t_valid),
        grid=(b, pairs, nb),
        in_specs=[smem, smem, smem, qk_spec, qk_spec, v_spec, v_spec,
                  pl.BlockSpec((1, tc, LANES), lambda bb, p, c: (bb, c, 0)),
                  c_spec, n_spec, pl.BlockSpec((1, ML_V_DIM), lambda bb, p, c: (0, 0))],
        out_specs=[v_spec, c_spec, n_spec, m_spec],
        out_shape=[jax.ShapeDtypeStruct((b, t, ML_V_WIDTH), F32),
                   jax.ShapeDtypeStruct((b, pairs, LANES, ML_V_DIM), F32),
                   jax.ShapeDtypeStruct((b, pairs, 1, LANES), F32),
                   jax.ShapeDtypeStruct((b, ML_HEADS, 1, LANES), F32)],
        scratch_shapes=[pltpu.VMEM((LANES, ML_V_DIM), F32), pltpu.VMEM((1, LANES), F32),
                        pltpu.VMEM((2, 1, LANES), F32)],
        compiler_params=_cparams(("parallel", "parallel", "arbitrary")),
        name="mlstm",
    )(i_bias, f_bias, m0, q, k, v, og, ifg,
      c0.reshape(b, pairs, LANES, ML_V_DIM), n0.reshape(b, pairs, 1, LANES), norm_w.reshape(1, ML_V_DIM))
    return (h, c_out.reshape(b, ML_HEADS, ML_QK_DIM, ML_V_DIM), n_out.reshape(b, ML_HEADS, ML_QK_DIM),
            m_out[:, :, 0, 0])


GROUP_LANE0 = N_EXPERTS
ROUTE_LANE0 = N_EXPERTS
PAIRS_PER_GROUP = EXPERTS_PER_GROUP * (EXPERTS_PER_GROUP - 1) // 2
N_CLASSES = N_GROUPS * PAIRS_PER_GROUP
MOE_TM = 256
SPARSE_MIN_TOKENS = 4096


def _router_kernel(x_ref, w_ref, b_ref, tri_ref, g_ref, n_ref, *rest):
    xr_ref, cnt_ref = rest[:-1], rest[-1]
    x1, x2, _ = _split3(x_ref[...])
    w1, w2, _ = _split3(w_ref[...])
    logits = _mm(x1, w1) + (_mm(x1, w2) + _mm(x2, w1)) + b_ref[...]
    tm = logits.shape[0]
    lane = _iota((tm, LANES), 1)
    lane_f = lane.astype(F32)
    is_group = (lane >= GROUP_LANE0) & (lane < GROUP_LANE0 + N_GROUPS)
    gl = jnp.where(is_group, logits, NEG_BIG)
    ge = jnp.where(is_group, jnp.exp(gl - jnp.max(gl, axis=-1, keepdims=True)), 0.0)
    gp = ge / jnp.sum(ge, axis=-1, keepdims=True)
    g_gate = jnp.max(gp, axis=-1, keepdims=True)
    g_idx = jnp.min(jnp.where(is_group & (gp == g_gate), lane_f, 1e9), axis=-1, keepdims=True) - GROUP_LANE0
    lo = g_idx * EXPERTS_PER_GROUP
    in_group = (lane_f >= lo) & (lane_f < lo + EXPERTS_PER_GROUP)
    fl = jnp.where(in_group, logits, NEG_BIG)
    fe = jnp.where(in_group, jnp.exp(fl - jnp.max(fl, axis=-1, keepdims=True)), 0.0)
    fp = fe / jnp.sum(fe, axis=-1, keepdims=True)
    w_a = jnp.max(jnp.where(in_group, fp, -1.0), axis=-1, keepdims=True)
    i_a = jnp.min(jnp.where(in_group & (fp == w_a), lane_f, 1e9), axis=-1, keepdims=True)
    rest = in_group & (lane_f != i_a)
    w_b = jnp.max(jnp.where(rest, fp, -1.0), axis=-1, keepdims=True)
    i_b = jnp.min(jnp.where(rest & (fp == w_b), lane_f, 1e9), axis=-1, keepdims=True)
    tot = w_a + w_b
    gate_a = g_gate * (w_a / tot)
    gate_b = g_gate * (w_b / tot)
    e_lo = jnp.minimum(i_a, i_b)
    e_hi = jnp.maximum(i_a, i_b)
    a_loc = e_lo - lo
    pair = a_loc * (7.0 - a_loc) * 0.5 + (e_hi - e_lo - 1.0)
    cls = g_idx * PAIRS_PER_GROUP + pair
    @pl.when(pl.program_id(0) == 0)
    def _():
        cnt_ref[...] = jnp.zeros_like(cnt_ref)

    onehot = jnp.where(lane_f == cls, 1.0, 0.0)
    earlier = _mm(tri_ref[...], onehot.astype(BF16)) + cnt_ref[0:1, :]
    rank = jnp.sum(onehot * earlier, axis=-1, keepdims=True)
    cnt_ref[...] = cnt_ref[...] + jnp.sum(onehot, axis=0, keepdims=True)
    n_ref[...] = cnt_ref[...]
    record = (e_lo, e_hi, jnp.where(i_a < i_b, gate_a, gate_b), jnp.where(i_a < i_b, gate_b, gate_a), cls, rank)
    out = jnp.where(lane_f == i_a, gate_a, jnp.where(lane_f == i_b, gate_b, 0.0))
    for off, val in enumerate(record):
        out = jnp.where(lane == ROUTE_LANE0 + off, val, out)
    g_ref[...] = out
    if xr_ref:
        d = x_ref.shape[1]
        xr_ref[0][:, :d] = x_ref[...]
        xr_ref[0][:, d:] = out


def _router(x, w_group, b_group, w_fine, b_fine, tm, with_rows):
    n, d = x.shape
    tm = min(tm, n)
    w = jnp.concatenate([w_fine.reshape(d, N_EXPERTS), w_group], axis=1)
    w = jnp.pad(w, ((0, 0), (0, LANES - w.shape[1])))
    bias = jnp.concatenate([b_fine.reshape(N_EXPERTS), b_group])
    bias = jnp.pad(bias, (0, LANES - bias.shape[0])).reshape(1, LANES)
    tri = (jnp.arange(tm)[:, None] > jnp.arange(tm)[None, :]).astype(BF16)
    out_specs = [pl.BlockSpec((tm, LANES), lambda i: (i, 0)), pl.BlockSpec((8, LANES), lambda i: (0, 0))]
    out_shape = [jax.ShapeDtypeStruct((n, LANES), F32), jax.ShapeDtypeStruct((8, LANES), F32)]
    if with_rows:
        out_specs.append(pl.BlockSpec((tm, d + LANES), lambda i: (i, 0)))
        out_shape.append(jax.ShapeDtypeStruct((n, d + LANES), F32))
    return pl.pallas_call(
        _router_kernel,
        grid=(n // tm,),
        in_specs=[pl.BlockSpec((tm, d), lambda i: (i, 0)),
                  pl.BlockSpec((d, LANES), lambda i: (0, 0)),
                  pl.BlockSpec((1, LANES), lambda i: (0, 0)),
                  pl.BlockSpec((tm, tm), lambda i: (0, 0))],
        out_specs=out_specs,
        out_shape=out_shape,
        scratch_shapes=[pltpu.VMEM((8, LANES), F32)],
        compiler_params=_cparams(("arbitrary",)),
        name="router",
    )(x, w, bias, tri)


def _moe_kernel(x_ref, gt_ref, wg_ref, wu_ref, wd_ref, g_ref, b_ref, o_ref, xb_ref, acc_ref):
    e = pl.program_id(1)

    @pl.when(e == 0)
    def _():
        xb_ref[...] = x_ref[...].astype(BF16)
        acc_ref[...] = jnp.zeros_like(acc_ref)

    gates = gt_ref[...]
    lane = _iota(gates.shape, 1)
    gcol = jnp.sum(jnp.where(lane == e, gates, 0.0), axis=-1, keepdims=True)
    xb = xb_ref[...]
    hg = _mm(xb, wg_ref[0])
    hu = _mm(xb, wu_ref[0])
    hh = (hg * _sigmoid(hg)) * hu * gcol
    acc_ref[...] += _mm(hh.astype(BF16), wd_ref[0])

    @pl.when(e == pl.num_programs(1) - 1)
    def _():
        o_ref[...] = _layer_norm_rows(DN_ALPHA * x_ref[...] + acc_ref[...], g_ref[...], b_ref[...])


def _moe_ln(x, gates, wg, wu, wd, g, b, tm):
    n, d = x.shape
    tm = min(tm, n)
    f = wg.shape[2]
    return pl.pallas_call(
        _moe_kernel,
        grid=(n // tm, N_EXPERTS),
        in_specs=[pl.BlockSpec((tm, d), lambda i, e: (i, 0)),
                  pl.BlockSpec((tm, LANES), lambda i, e: (i, 0)),
                  pl.BlockSpec((1, d, f), lambda i, e: (e, 0, 0)),
                  pl.BlockSpec((1, d, f), lambda i, e: (e, 0, 0)),
                  pl.BlockSpec((1, f, d), lambda i, e: (e, 0, 0)),
                  pl.BlockSpec((1, d), lambda i, e: (0, 0)),
                  pl.BlockSpec((1, d), lambda i, e: (0, 0))],
        out_specs=pl.BlockSpec((tm, d), lambda i, e: (i, 0)),
        out_shape=jax.ShapeDtypeStruct((n, d), F32),
        scratch_shapes=[pltpu.VMEM((tm, d), BF16), pltpu.VMEM((tm, d), F32)],
        compiler_params=_cparams(("parallel", "arbitrary")),
        name="moe_ln",
    )(x, gates, wg, wu, wd, g.reshape(1, d), b.reshape(1, d))


def _route_plan(cls, rank, counts, n, tm):
    n_tiles = (n + N_CLASSES * tm) // tm
    pairs = [(a, b) for a in range(EXPERTS_PER_GROUP) for b in range(a + 1, EXPERTS_PER_GROUP)]
    lo_tab = jnp.asarray([EXPERTS_PER_GROUP * g + a for g in range(N_GROUPS) for a, _ in pairs], jnp.int32)
    hi_tab = jnp.asarray([EXPERTS_PER_GROUP * g + b for g in range(N_GROUPS) for _, b in pairs], jnp.int32)
    classes = jnp.arange(N_CLASSES, dtype=jnp.int32)
    padded = (counts + tm - 1) // tm * tm
    p_end = jnp.sum(jnp.where(classes[None, :] <= classes[:, None], padded[None, :], 0), axis=1)
    p_start = p_end - padded
    pos = jnp.sum(jnp.where(cls[:, None] == classes[None, :], p_start[None, :], 0), axis=1) + rank
    last_tile_row = jnp.where(padded > 0, p_end - tm, -1)
    tile_row0 = jnp.arange(n_tiles, dtype=jnp.int32) * tm
    tile_cls = jnp.minimum(jnp.sum((p_end[None, :] <= tile_row0[:, None]).astype(jnp.int32), axis=1), N_CLASSES - 1)
    pick = lambda tab: jnp.sum(jnp.where(tile_cls[:, None] == classes[None, :], tab[None, :], 0), axis=1)
    n_used = (p_end[N_CLASSES - 1] // tm).reshape(1)
    return pos.astype(jnp.int32), last_tile_row.astype(jnp.int32), pick(lo_tab), pick(hi_tab), n_used.astype(jnp.int32)


def _row_copies(src_row, dst_row, sems, tm):
    def issue(r2, carry):
        for prio in range(2):
            r = 2 * r2 + prio
            pltpu.make_async_copy(src_row(r), dst_row(r), sems.at[prio]).start(priority=prio)
        return carry
    lax.fori_loop(0, tm // 2, issue, 0, unroll=4)
    for prio in range(2):
        pltpu.make_async_copy(src_row(0, tm // 2), dst_row(0, tm // 2), sems.at[prio]).wait()


def _dispatch_rows_kernel(pos_ref, last_ref, nused_ref, xr_ref, xs_hbm, zero_ref, sems, *, tm, n_tiles):
    i = pl.program_id(0)

    @pl.when(i == 0)
    def _():
        zero_ref[...] = jnp.zeros_like(zero_ref)

        def fill(row0):
            cp = pltpu.make_async_copy(zero_ref, xs_hbm.at[pl.ds(pl.multiple_of(row0, tm), tm)], sems.at[0])
            cp.start()
            cp.wait()

        for c in range(N_CLASSES):
            @pl.when(last_ref[c] >= 0)
            def _():
                fill(last_ref[c])

        def fill_unused(t, carry):
            fill(t * tm)
            return carry
        lax.fori_loop(nused_ref[0], n_tiles, fill_unused, 0)

    _row_copies(lambda r, rows=1: xr_ref.at[pl.ds(r, rows)],
                lambda r, rows=1: xs_hbm.at[pl.ds(pos_ref[i * tm + r] if rows == 1 else 0, rows)], sems, tm)


def _combine_ln_kernel(pos_ref, ys_hbm, x_ref, g_ref, b_ref, o_ref, buf_ref, sems, *, tm):
    i = pl.program_id(0)
    _row_copies(lambda r, rows=1: ys_hbm.at[pl.ds(pos_ref[i * tm + r] if rows == 1 else 0, rows)],
                lambda r, rows=1: buf_ref.at[pl.ds(r, rows)], sems, tm)
    o_ref[...] = _layer_norm_rows(DN_ALPHA * x_ref[...] + buf_ref[...], g_ref[...], b_ref[...])


def _expert_pair_kernel(lo_ref, hi_ref, nused_ref, xs_ref, wg_lo, wu_lo, wd_lo, wg_hi, wu_hi, wd_hi, y_ref, *, d):
    i = pl.program_id(0)

    @pl.when(i < nused_ref[0])
    def _():
        x16 = xs_ref[:, :d].astype(BF16)
        y = jnp.zeros(y_ref.shape, F32)
        for off, (wg, wu, wd) in ((2, (wg_lo, wu_lo, wd_lo)), (3, (wg_hi, wu_hi, wd_hi))):
            gate = xs_ref[:, d + ROUTE_LANE0 + off:d + ROUTE_LANE0 + off + 1]
            hg = _mm(x16, wg[0])
            hu = _mm(x16, wu[0])
            y = y + _mm(((hg * _sigmoid(hg)) * hu * gate).astype(BF16), wd[0])
        y_ref[...] = y

    @pl.when(i >= nused_ref[0])
    def _():
        y_ref[...] = jnp.zeros_like(y_ref)


def _moe_ln_sparse(x, gates, counts, xr, wg, wu, wd, g, b):
    n, d = x.shape
    tm = MOE_TM
    f = wg.shape[2]
    record = gates[:, ROUTE_LANE0 + 4:ROUTE_LANE0 + 6].astype(jnp.int32)
    pos, last_tile_row, tile_lo, tile_hi, n_used = _route_plan(
        record[:, 0], record[:, 1], counts[0, :N_CLASSES].astype(jnp.int32), n, tm)
    n_tiles = (n + N_CLASSES * tm) // tm
    n_rows = n_tiles * tm
    two_sems = pltpu.SemaphoreType.DMA((2,))
    xs = pl.pallas_call(
        functools.partial(_dispatch_rows_kernel, tm=tm, n_tiles=n_tiles),
        grid_spec=pltpu.PrefetchScalarGridSpec(
            num_scalar_prefetch=3, grid=(n // tm,),
            in_specs=[pl.BlockSpec((tm, d + LANES), lambda i, ps, la, u: (i, 0))],
            out_specs=pl.BlockSpec(memory_space=pl.ANY),
            scratch_shapes=[pltpu.VMEM((tm, d + LANES), F32), two_sems]),
        out_shape=jax.ShapeDtypeStruct((n_rows, d + LANES), F32),
        compiler_params=_cparams(("arbitrary",)),
        name="moe_dispatch",
    )(pos, last_tile_row, n_used, xr)
    wspec = lambda which, shape: pl.BlockSpec(
        (1,) + shape, (lambda i, lo, hi, u: (lo[i], 0, 0)) if which == 0 else (lambda i, lo, hi, u: (hi[i], 0, 0)))
    ys = pl.pallas_call(
        functools.partial(_expert_pair_kernel, d=d),
        grid_spec=pltpu.PrefetchScalarGridSpec(
            num_scalar_prefetch=3, grid=(n_tiles,),
            in_specs=[pl.BlockSpec((tm, d + LANES), lambda i, lo, hi, u: (i, 0)),
                      wspec(0, (d, f)), wspec(0, (d, f)), wspec(0, (f, d)),
                      wspec(1, (d, f)), wspec(1, (d, f)), wspec(1, (f, d))],
            out_specs=pl.BlockSpec((tm, d), lambda i, lo, hi, u: (i, 0))),
        out_shape=jax.ShapeDtypeStruct((n_rows, d), F32),
        compiler_params=_cparams(("arbitrary",)),
        name="moe_experts",
    )(tile_lo, tile_hi, n_used, xs, wg, wu, wd, wg, wu, wd)
    return pl.pallas_call(
        functools.partial(_combine_ln_kernel, tm=tm),
        grid_spec=pltpu.PrefetchScalarGridSpec(
            num_scalar_prefetch=1, grid=(n // tm,),
            in_specs=[pl.BlockSpec(memory_space=pl.ANY),
                      pl.BlockSpec((tm, d), lambda i, ps: (i, 0)),
                      pl.BlockSpec((1, d), lambda i, ps: (0, 0)),
                      pl.BlockSpec((1, d), lambda i, ps: (0, 0))],
            out_specs=pl.BlockSpec((tm, d), lambda i, ps: (i, 0)),
            scratch_shapes=[pltpu.VMEM((tm, d), F32), two_sems]),
        out_shape=jax.ShapeDtypeStruct((n, d), F32),
        compiler_params=_cparams(("arbitrary",)),
        name="moe_combine_ln",
    )(pos, ys, x, g.reshape(1, d), b.reshape(1, d))


EVEN_WIDTHS = (SB_WIDTH, SB_WIDTH, SB_WIDTH, 3 * GDN_WIDTH, GDN_WIDTH, LANES)
ODD_WIDTHS = (ML_QK_WIDTH, ML_QK_WIDTH, ML_V_WIDTH, ML_V_WIDTH, LANES)
ROW_TILE = 512


def _pad_cols(w, total):
    return jnp.pad(w, ((0, 0), (0, total - w.shape[1])))


def _pad_tokens(a, t_pad):
    return jnp.pad(a, ((0, 0), (0, t_pad - a.shape[1])) + ((0, 0),) * (a.ndim - 2))


def _run_trunk(x, past, conv_buf, gdn_state, ml_state, p):
    b, t, d = x.shape
    n = b * t
    t_pad = max(t, MIN_CHUNK)
    xt = x.reshape(n, d)

    q_sb, k_sb, v_sb, qkv_g, z_g, ab = _project(xt, p['even_w_in'], EVEN_WIDTHS, ROW_TILE)
    q3, k3, v3 = (a.reshape(b, t, SB_WIDTH) for a in (q_sb, k_sb, v_sb))
    if past is None:
        o_sb = _sb_prompt(q3, k3, v3, p['sb_bias'])
    else:
        o_sb = _sb_sample(q3, k3, v3, past[0], past[1], past[2], p['sb_bias'])
    qkv3 = qkv_g.reshape(b, t, 3 * GDN_WIDTH)
    conv_prev8 = jnp.pad(conv_buf, ((0, 0), (8 - (GDN_CONV - 1), 0), (0, 0)))
    o_g, s_out = _gdn(_pad_tokens(qkv3, t_pad), _pad_tokens(z_g.reshape(b, t, GDN_WIDTH), t_pad),
                      _pad_tokens(ab.reshape(b, t, LANES), t_pad), conv_prev8, gdn_state,
                      p['gdn_conv_w'], p['gdn_a_log'], p['gdn_dt_bias'], p['gdn_norm_w'], t)
    new_buf = jnp.concatenate([conv_buf, qkv3], axis=1)[:, -(GDN_CONV - 1):]
    merged = jnp.concatenate([o_sb, o_g[:, :t]], axis=-1).reshape(n, SB_WIDTH + GDN_WIDTH)
    xt = _outproj_ln(merged, p['even_w_out'], xt, p['ln_mix_g'][0], p['ln_mix_b'][0], ROW_TILE)
    xt = _ffn(xt, p, 0)

    q_m, k_m, v_m, o_m, ifg = _project(xt, p['odd_w_in'], ODD_WIDTHS, ROW_TILE)
    c0, n0, m0 = ml_state
    h_m, c_out, n_out, m_out = _mlstm(
        _pad_tokens(q_m.reshape(b, t, ML_QK_WIDTH), t_pad), _pad_tokens(k_m.reshape(b, t, ML_QK_WIDTH), t_pad),
        _pad_tokens(v_m.reshape(b, t, ML_V_WIDTH), t_pad), _pad_tokens(o_m.reshape(b, t, ML_V_WIDTH), t_pad),
        _pad_tokens(ifg.reshape(b, t, LANES), t_pad), c0, n0, m0,
        p['mlstm_i_bias'], p['mlstm_f_bias'], p['mlstm_norm_w'], t)
    xt = _outproj_ln(h_m[:, :t].reshape(n, ML_V_WIDTH), p['odd_w_out'], xt,
                     p['ln_mix_g'][1], p['ln_mix_b'][1], ROW_TILE)
    xt = _ffn(xt, p, 1)

    kv_shape = (1, b, t, SB_HEADS, SB_HEAD_DIM)
    return (xt.reshape(b, t, d), k_sb.reshape(kv_shape), v_sb.reshape(kv_shape), new_buf[None], s_out[None],
            c_out[None], n_out[None], m_out[None])


def _ffn(xt, p, layer):
    sparse = xt.shape[0] >= SPARSE_MIN_TOKENS
    routed = _router(xt, p['moe_w_group'][layer], p['moe_b_group'][layer], p['moe_w_fine'][layer],
                     p['moe_b_fine'][layer], ROW_TILE, sparse)
    weights = (p['moe_w_gate'][layer], p['moe_w_up'][layer], p['moe_w_down'][layer],
               p['ln_ffn_g'][layer], p['ln_ffn_b'][layer])
    if sparse:
        return _moe_ln_sparse(xt, routed[0], routed[1], routed[2], *weights)
    return _moe_ln(xt, routed[0], *weights, ROW_TILE)


def kernel(x_prompt, x_sample, cache_k, cache_v, state_conv, state_gdn, state_mlstm_C, state_mlstm_n,
           state_mlstm_m, page_table, even_w_in, even_w_out, sb_bias, gdn_conv_w, gdn_a_log, gdn_dt_bias,
           gdn_norm_w, odd_w_in, odd_w_out, mlstm_i_bias, mlstm_f_bias, mlstm_norm_w, ln_mix_g, ln_mix_b,
           ln_ffn_g, ln_ffn_b, moe_w_group, moe_b_group, moe_w_fine, moe_b_fine, moe_w_gate, moe_w_up,
           moe_w_down):
    assert DEPTH == 2 and even_w_in.shape[0] == 1 and odd_w_in.shape[0] == 1
    p = {
        'even_w_in': _pad_cols(even_w_in[0], sum(EVEN_WIDTHS)).astype(BF16),
        'even_w_out': even_w_out[0].astype(BF16),
        'sb_bias': sb_bias[0], 'gdn_conv_w': gdn_conv_w[0], 'gdn_a_log': gdn_a_log[0],
        'gdn_dt_bias': gdn_dt_bias[0], 'gdn_norm_w': gdn_norm_w[0],
        'odd_w_in': _pad_cols(odd_w_in[0], sum(ODD_WIDTHS)).astype(BF16),
        'odd_w_out': odd_w_out[0].astype(BF16),
        'mlstm_i_bias': mlstm_i_bias[0], 'mlstm_f_bias': mlstm_f_bias[0], 'mlstm_norm_w': mlstm_norm_w[0],
        'ln_mix_g': ln_mix_g, 'ln_mix_b': ln_mix_b, 'ln_ffn_g': ln_ffn_g, 'ln_ffn_b': ln_ffn_b,
        'moe_w_group': moe_w_group, 'moe_b_group': moe_b_group, 'moe_w_fine': moe_w_fine,
        'moe_b_fine': moe_b_fine, 'moe_w_gate': moe_w_gate.astype(BF16), 'moe_w_up': moe_w_up.astype(BF16),
        'moe_w_down': moe_w_down.astype(BF16),
    }
    bp = x_prompt.shape[0]
    zero_buf = jnp.zeros((bp, GDN_CONV - 1, 3 * GDN_WIDTH), F32)
    zero_s = jnp.zeros((bp, GDN_HEADS, GDN_HEAD_DIM, GDN_HEAD_DIM), F32)
    zero_ml = (jnp.zeros((bp, ML_HEADS, ML_QK_DIM, ML_V_DIM), F32), jnp.zeros((bp, ML_HEADS, ML_QK_DIM), F32),
               jnp.zeros((bp, ML_HEADS), F32))
    (y_p, k_p, v_p, conv_p, gdn_p, mc_p, mn_p, mm_p) = _run_trunk(x_prompt, None, zero_buf, zero_s, zero_ml, p)

    pool_k = jnp.transpose(cache_k.reshape(cache_k.shape[1:]), (0, 2, 3, 1))
    pool_v = jnp.transpose(cache_v.reshape(cache_v.shape[1:]), (0, 2, 3, 1))
    (y_s, k_s, v_s, conv_s, gdn_s, mc_s, mn_s, mm_s) = _run_trunk(
        x_sample, (pool_k, pool_v, page_table), state_conv[0], state_gdn[0],
        (state_mlstm_C[0], state_mlstm_n[0], state_mlstm_m[0]), p)
    return (y_p, y_s, k_p, v_p, k_s, v_s, conv_p, conv_s, gdn_p, gdn_s, mc_p, mc_s, mn_p, mn_s, mm_p, mm_s)
```

```python
import functools
import math

import jax
import jax.numpy as jnp
from jax import lax
from jax.experimental import pallas as pl
from jax.experimental.pallas import tpu as pltpu

F32 = jnp.float32
BF16 = jnp.bfloat16

D_MODEL = 1024
DEPTH = 2
PAGE_SIZE = 128
SB_HEADS = 8
SB_HEAD_DIM = 64
SB_WIDTH = SB_HEADS * SB_HEAD_DIM
SB_SCALE = SB_HEAD_DIM ** -0.5
GDN_HEADS = 4
GDN_HEAD_DIM = 128
GDN_WIDTH = GDN_HEADS * GDN_HEAD_DIM
GDN_CONV = 4
ML_HEADS = 8
ML_QK_DIM = 64
ML_V_DIM = 128
ML_QK_WIDTH = ML_HEADS * ML_QK_DIM
ML_V_WIDTH = ML_HEADS * ML_V_DIM
N_GROUPS = 4
EXPERTS_PER_GROUP = 4
N_EXPERTS = N_GROUPS * EXPERTS_PER_GROUP
EXPERT_FF = 512
DN_ALPHA = (2 * DEPTH) ** 0.25
LN_EPS = 1e-5
RMS_EPS = 1e-6
L2_EPS = 1e-6

LANES = 128
MAX_CHUNK = 64
MIN_CHUNK = 16
NEG_BIG = -1e30
VMEM_LIMIT = 56 * 1024 * 1024


def _cparams(sem):
    return pltpu.CompilerParams(dimension_semantics=sem, vmem_limit_bytes=VMEM_LIMIT)


def _mm(a, b):
    return jnp.dot(a, b, preferred_element_type=F32)


def _mm_nt(a, b):
    return lax.dot_general(a, b, (((1,), (1,)), ((), ())), preferred_element_type=F32)


def _mm_tn(a, b):
    return lax.dot_general(a, b, (((0,), (0,)), ((), ())), preferred_element_type=F32)


def _split3(x):
    x1 = x.astype(BF16)
    r1 = x - x1.astype(F32)
    x2 = r1.astype(BF16)
    x3 = (r1 - x2.astype(F32)).astype(BF16)
    return x1, x2, x3


def _mm_exact_lhs(a_bf16, b):
    b1, b2, b3 = _split3(b)
    return _mm(a_bf16, b1) + _mm(a_bf16, b2) + _mm(a_bf16, b3)


def _mm3(a, b):
    a1, a2, _ = _split3(a)
    b1, b2, _ = _split3(b)
    return _mm(a1, b1) + (_mm(a1, b2) + _mm(a2, b1))


def _softplus(x):
    return jnp.maximum(x, 0.0) + jnp.log1p(jnp.exp(-jnp.abs(x)))


def _sigmoid(x):
    return jax.nn.sigmoid(x)


def _iota(shape, dim):
    return lax.broadcasted_iota(jnp.int32, shape, dim)


def _proj_kernel(x_ref, w_ref, *out_refs, widths):
    x = x_ref[...].astype(BF16)
    off = 0
    for o_ref, wd in zip(out_refs, widths):
        o_ref[...] = _mm(x, w_ref[:, off:off + wd])
        off += wd


def _project(x, w_bf16, widths, tm):
    n, d = x.shape
    tm = min(tm, n)
    total = sum(widths)
    return pl.pallas_call(
        functools.partial(_proj_kernel, widths=tuple(widths)),
        grid=(n // tm,),
        in_specs=[pl.BlockSpec((tm, d), lambda i: (i, 0)),
                  pl.BlockSpec((d, total), lambda i: (0, 0))],
        out_specs=[pl.BlockSpec((tm, wd), lambda i: (i, 0)) for wd in widths],
        out_shape=[jax.ShapeDtypeStruct((n, wd), F32) for wd in widths],
        compiler_params=_cparams(("parallel",)),
        name="proj",
    )(x, w_bf16)


def _layer_norm_rows(r, g, b):
    mu = jnp.mean(r, axis=-1, keepdims=True)
    c = r - mu
    var = jnp.mean(c * c, axis=-1, keepdims=True)
    return c * lax.rsqrt(var + LN_EPS) * g + b


def _outproj_ln_kernel(m_ref, w_ref, x_ref, g_ref, b_ref, o_ref):
    y = _mm(m_ref[...].astype(BF16), w_ref[...])
    o_ref[...] = _layer_norm_rows(DN_ALPHA * x_ref[...] + y, g_ref[...], b_ref[...])


def _outproj_ln(merged, w_bf16, x, g, b, tm):
    n, k = merged.shape
    d = x.shape[1]
    tm = min(tm, n)
    return pl.pallas_call(
        _outproj_ln_kernel,
        grid=(n // tm,),
        in_specs=[pl.BlockSpec((tm, k), lambda i: (i, 0)),
                  pl.BlockSpec((k, d), lambda i: (0, 0)),
                  pl.BlockSpec((tm, d), lambda i: (i, 0)),
                  pl.BlockSpec((1, d), lambda i: (0, 0)),
                  pl.BlockSpec((1, d), lambda i: (0, 0))],
        out_specs=pl.BlockSpec((tm, d), lambda i: (i, 0)),
        out_shape=jax.ShapeDtypeStruct((n, d), F32),
        compiler_params=_cparams(("parallel",)),
        name="outproj_ln",
    )(merged, w_bf16, x, g.reshape(1, d), b.reshape(1, d))


SB_SUB = 256
SB_TQ = 512
SB_TKB = 1024


LOG2E = math.log2(math.e)


def _suffix_sum_matrix(n):
    return (jnp.arange(n)[:, None] > jnp.arange(n)[None, :]).astype(BF16)


def _sb_chains(chains, u):
    logs = [[jnp.minimum(nz, 0.0) - jnp.log(1.0 + jnp.exp2(jnp.abs(nz) * -LOG2E)) for nz in tiles]
            for tiles, _, _ in chains]
    masked = [[l if c is None else jnp.where(c, l, 0.0) for l, c in zip(ls, causals)]
              for ls, (_, causals, _) in zip(logs, chains)]
    sums = [[_mm(lm.astype(BF16), u) for lm in lms] for lms in masked]
    out = []
    for ls, lms, ss, (tiles, causals, r_b) in zip(logs, masked, sums, chains):
        ws = []
        for nz, l, lm, s, c in zip(tiles, ls, lms, ss, causals):
            tk = nz.shape[1]
            tail = s + jnp.concatenate([r_b] * (tk // LANES), axis=1)
            w = jnp.exp2(((l - nz) + tail) * LOG2E)
            ws.append(w if c is None else jnp.where(c, w, 0.0))
            r_b = r_b + jnp.broadcast_to(s[:, :1] + lm[:, :1], r_b.shape)
        out.append((ws, r_b))
    return out


def _sb_prompt_kernel(qi_ref, kb_ref, bias_ref, u_ref, q_ref, k_ref, v_ref, o_ref, acc_ref, r_ref,
                      *, tq, tkb, sub):
    hp = pl.program_id(1)
    p = pl.program_id(2)
    q0 = qi_ref[p] * tq
    k0 = kb_ref[p] * tkb

    @pl.when(k0 + tkb >= q0 + tq)
    def _():
        acc_ref[...] = jnp.zeros_like(acc_ref)
        r_ref[...] = jnp.zeros_like(r_ref)

    def body(masked):
        lane = _iota((1, LANES), 1)
        nq2 = q_ref[0] * (-SB_SCALE)
        u = u_ref[...]
        ones_blk = jnp.where(_iota((sub, LANES), 1) < 3, 1.0, 0.0).astype(BF16)
        subs = range(tkb // sub - 1, -1, -1)
        k_ext = [jnp.concatenate([k_ref[0, s * sub:(s + 1) * sub, :].astype(BF16), ones_blk], axis=1) for s in subs]
        causals = [None] * len(subs)
        if masked:
            causals = [(_iota((tq, sub), 1) + (k0 + s * sub)) < (_iota((tq, sub), 0) + q0) for s in subs]
        chains, in_heads = [], []
        for e in range(2):
            in_head = (lane >= SB_HEAD_DIM * e) & (lane < SB_HEAD_DIM * (e + 1))
            nqh = jnp.where(in_head, nq2, 0.0).astype(BF16)
            nb = [part.astype(F32) for part in _split3(jnp.full((1, LANES), -bias_ref[2 * hp + e], F32))]
            bias_row = jnp.where(lane == 0, nb[0], jnp.where(lane == 1, nb[1], jnp.where(lane == 2, nb[2], 0.0)))
            nq_ext = jnp.concatenate([nqh, jnp.broadcast_to(bias_row, (tq, LANES)).astype(BF16)], axis=1)
            chains.append(([_mm_nt(nq_ext, ks) for ks in k_ext], causals, r_ref[e]))
            in_heads.append(in_head)
        upd = jnp.zeros((tq, LANES), F32)
        for e, (ws, r_b) in enumerate(_sb_chains(chains, u)):
            r_ref[e] = r_b
            for w, s in zip(ws, subs):
                vs = jnp.where(in_heads[e], v_ref[0, s * sub:(s + 1) * sub, :], 0.0).astype(BF16)
                upd = upd + _mm(w.astype(BF16), vs)
        acc_ref[...] += upd

    needs_mask = k0 + tkb > q0

    @pl.when(needs_mask)
    def _():
        body(True)

    @pl.when(jnp.logical_not(needs_mask))
    def _():
        body(False)

    @pl.when(k0 == 0)
    def _():
        o_ref[0] = acc_ref[...]


def _sb_prompt(q, k, v, bias):
    b, t, _ = q.shape
    tq, tkb = min(SB_TQ, t), min(SB_TKB, t)
    sub = min(SB_SUB, tkb)
    qi_list, kb_list = [], []
    for i in range(t // tq):
        for kb in range(-(-(i + 1) * tq // tkb) - 1, -1, -1):
            qi_list.append(i)
            kb_list.append(kb)
    qi = jnp.asarray(qi_list, jnp.int32)
    kb = jnp.asarray(kb_list, jnp.int32)
    grid_spec = pltpu.PrefetchScalarGridSpec(
        num_scalar_prefetch=2,
        grid=(b, SB_HEADS // 2, len(qi_list)),
        in_specs=[pl.BlockSpec(memory_space=pltpu.SMEM),
                  pl.BlockSpec((sub, sub), lambda bb, hp, p, qi, kb: (0, 0)),
                  pl.BlockSpec((1, tq, LANES), lambda bb, hp, p, qi, kb: (bb, qi[p], hp)),
                  pl.BlockSpec((1, tkb, LANES), lambda bb, hp, p, qi, kb: (bb, kb[p], hp)),
                  pl.BlockSpec((1, tkb, LANES), lambda bb, hp, p, qi, kb: (bb, kb[p], hp))],
        out_specs=pl.BlockSpec((1, tq, LANES), lambda bb, hp, p, qi, kb: (bb, qi[p], hp)),
        scratch_shapes=[pltpu.VMEM((tq, LANES), F32), pltpu.VMEM((2, tq, LANES), F32)],
    )
    return pl.pallas_call(
        functools.partial(_sb_prompt_kernel, tq=tq, tkb=tkb, sub=sub),
        grid_spec=grid_spec,
        out_shape=jax.ShapeDtypeStruct((b, t, SB_WIDTH), F32),
        compiler_params=_cparams(("parallel", "parallel", "arbitrary")),
        name="sb_prompt",
    )(qi, kb, bias, _suffix_sum_matrix(sub), q, k, v)


PAGES_PER_STEP = 16


def _sb_sample_kernel(pt_ref, bias_ref, u_ref, q_ref, kn_ref, vn_ref, *rest, t_new):
    k_refs = rest[:PAGES_PER_STEP]
    v_refs = rest[PAGES_PER_STEP:2 * PAGES_PER_STEP]
    o_ref, nq_ref, nb_ref, acc_ref, r_ref = rest[2 * PAGES_PER_STEP:]
    rows = SB_HEADS * t_new
    j = pl.program_id(1)

    def process(page_k_refs, page_v_refs, masked):
        n_pg = len(page_k_refs)
        keys = lambda refs, h: jnp.concatenate([ref[0, h].astype(BF16) for ref in refs], axis=1)
        nz = jnp.concatenate([_mm(nq_ref[h], keys(page_k_refs, h)) for h in range(SB_HEADS)], axis=0)
        nz = nz + jnp.concatenate([nb_ref[...]] * n_pg, axis=1)
        causal = None
        if masked:
            causal = _iota((rows, PAGE_SIZE), 1) < (_iota((rows, PAGE_SIZE), 0) % t_new)
        tiles = [nz[:, g * PAGE_SIZE:(g + 1) * PAGE_SIZE] for g in range(n_pg)]
        (ws, r_b), = _sb_chains([(tiles, [causal] * n_pg, r_ref[...])], u_ref[...])
        r_ref[...] = r_b
        w = jnp.concatenate(ws, axis=1)
        for h in range(SB_HEADS):
            acc_ref[h] += _mm_nt(w[h * t_new:(h + 1) * t_new].astype(BF16), keys(page_v_refs, h))

    @pl.when(j == 0)
    def _():
        nq_ref[...] = (q_ref[0] * (-SB_SCALE)).astype(BF16)
        hrow = _iota((rows, LANES), 0) // t_new
        nb = jnp.zeros((rows, LANES), F32)
        for h in range(SB_HEADS):
            nb = jnp.where(hrow == h, -bias_ref[h], nb)
        nb_ref[...] = nb
        acc_ref[...] = jnp.zeros_like(acc_ref)
        r_ref[...] = jnp.zeros_like(r_ref)
        process([kn_ref], [vn_ref], True)

    process(k_refs, v_refs, False)

    @pl.when(j == pl.num_programs(1) - 1)
    def _():
        o_ref[0] = acc_ref[...]


def _sb_sample(q, k_new, v_new, pool_k, pool_v, page_table, bias):
    b, t_new, _ = q.shape
    n_pages = page_table.shape[1]
    steps = n_pages // PAGES_PER_STEP
    rows = SB_HEADS * t_new
    page_block = (1, SB_HEADS, SB_HEAD_DIM, PAGE_SIZE)

    def as_page(a):
        a = a.reshape(b, t_new, SB_HEADS, SB_HEAD_DIM).transpose(0, 2, 3, 1)
        return jnp.pad(a, ((0, 0), (0, 0), (0, 0), (0, PAGE_SIZE - t_new)))

    q_heads = q.reshape(b, t_new, SB_HEADS, SB_HEAD_DIM).transpose(0, 2, 1, 3)

    def page_spec(r):
        def imap(bb, j, pt):
            return (pt[bb, n_pages - 1 - (j * PAGES_PER_STEP + r)], 0, 0, 0)
        return pl.BlockSpec(page_block, imap)

    new_spec = pl.BlockSpec(page_block, lambda bb, j, pt: (bb, 0, 0, 0))
    qo_spec = pl.BlockSpec((1, SB_HEADS, t_new, SB_HEAD_DIM), lambda bb, j, pt: (bb, 0, 0, 0))
    grid_spec = pltpu.PrefetchScalarGridSpec(
        num_scalar_prefetch=1,
        grid=(b, steps),
        in_specs=([pl.BlockSpec(memory_space=pltpu.SMEM),
                   pl.BlockSpec((PAGE_SIZE, PAGE_SIZE), lambda bb, j, pt: (0, 0)),
                   qo_spec, new_spec, new_spec]
                  + [page_spec(r) for r in range(PAGES_PER_STEP)]
                  + [page_spec(r) for r in range(PAGES_PER_STEP)]),
        out_specs=qo_spec,
        scratch_shapes=[pltpu.VMEM((SB_HEADS, t_new, SB_HEAD_DIM), BF16), pltpu.VMEM((rows, LANES), F32),
                        pltpu.VMEM((SB_HEADS, t_new, SB_HEAD_DIM), F32), pltpu.VMEM((rows, LANES), F32)],
    )
    o = pl.pallas_call(
        functools.partial(_sb_sample_kernel, t_new=t_new),
        grid_spec=grid_spec,
        out_shape=jax.ShapeDtypeStruct((b, SB_HEADS, t_new, SB_HEAD_DIM), F32),
        compiler_params=_cparams(("parallel", "arbitrary")),
        name="sb_sample",
    )(page_table, bias, _suffix_sum_matrix(PAGE_SIZE), q_heads, as_page(k_new), as_page(v_new),
      *([pool_k] * PAGES_PER_STEP), *([pool_v] * PAGES_PER_STEP))
    return o.transpose(0, 2, 1, 3).reshape(b, t_new, SB_WIDTH)


def _chunk_masks(ch):
    i = _iota((ch, ch), 0)
    j = _iota((ch, ch), 1)
    return i >= j, i > j


def _gdn_prep_kernel(alog_ref, dtb_ref, qr_ref, kr_ref, vr_ref, qp_ref, kp_ref, vp_ref,
                     qs_ref, ks_ref, vs_ref, wq_ref, wk_ref, wv_ref, ab_ref,
                     u_ref, w_ref, qg_ref, kd_ref, sd_ref, eg_ref, xbuf_ref, *, tc, ch, t_valid):
    h = pl.program_id(1)
    c = pl.program_id(2)

    def conv_silu(x_ref, prev_ref, state_ref, wt_ref):
        first = (c == 0).astype(F32)
        xbuf_ref[0:8, :] = first * state_ref[0] + (1.0 - first) * prev_ref[0]
        xbuf_ref[8:, :] = x_ref[0]
        y = jnp.zeros((tc, LANES), F32)
        for i in range(GDN_CONV):
            y = y + xbuf_ref[5 + i:5 + i + tc, :] * wt_ref[i:i + 1, :]
        return y * _sigmoid(y)

    qc = conv_silu(qr_ref, qp_ref, qs_ref, wq_ref)
    kc = conv_silu(kr_ref, kp_ref, ks_ref, wk_ref)
    vc = conv_silu(vr_ref, vp_ref, vs_ref, wv_ref)
    q = qc * lax.rsqrt(jnp.sum(qc * qc, axis=-1, keepdims=True) + L2_EPS) * GDN_HEAD_DIM ** -0.5
    k = kc * lax.rsqrt(jnp.sum(kc * kc, axis=-1, keepdims=True) + L2_EPS)

    ab = ab_ref[0]
    lane = _iota((tc, LANES), 1)
    a_b = jnp.broadcast_to(jnp.sum(jnp.where(lane == h, ab, 0.0), axis=-1, keepdims=True), (tc, LANES))
    b_b = jnp.broadcast_to(jnp.sum(jnp.where(lane == GDN_HEADS + h, ab, 0.0), axis=-1, keepdims=True),
                           (tc, LANES))
    neg_a = -jnp.exp(jnp.full((1, LANES), alog_ref[h], F32))
    g_b = neg_a * _softplus(a_b + dtb_ref[h])
    beta_b = _sigmoid(b_b)
    valid = (_iota((tc, LANES), 0) + c * tc) < t_valid
    g_b = jnp.where(valid, g_b, 0.0)
    beta_b = jnp.where(valid, beta_b, 0.0)

    lower, strict = _chunk_masks(ch)
    t1 = lower.astype(BF16)
    t2 = (_iota((ch, ch), 0) > _iota((ch, ch), 1)).astype(F32)
    eye = (_iota((ch, ch), 0) == _iota((ch, ch), 1)).astype(F32)

    chunks = range(tc // ch)
    sls = [slice(n * ch, (n + 1) * ch) for n in chunks]
    gc_b = [_mm_exact_lhs(t1, g_b[sl]) for sl in sls]
    dg = [_mm_exact_lhs(t1, g_b[sl][:, :ch] * t2) for sl in sls]
    decay = [jnp.where(lower, jnp.exp(jnp.where(lower, d, 0.0)), 0.0) for d in dg]
    kbeta = [k[sl] * beta_b[sl] for sl in sls]
    k16 = [k[sl].astype(BF16) for sl in sls]
    a = [jnp.where(strict, _mm_nt(kbeta[n].astype(BF16), k16[n]) * decay[n], 0.0) for n in chunks]
    tinv = [eye - a[n] for n in chunks]
    pw = a
    for _ in range(int(math.log2(ch)) - 1):
        pw = [_mm3(pw[n], pw[n]) for n in chunks]
        tinv = [tinv[n] + _mm3(tinv[n], pw[n]) for n in chunks]
    sol = [_mm3(tinv[n], jnp.concatenate([vc[sls[n]] * beta_b[sls[n]], kbeta[n] * jnp.exp(gc_b[n])], axis=1))
           for n in chunks]
    for n in chunks:
        sl = sls[n]
        u_ref[0, 0, sl, :] = sol[n][:, :GDN_HEAD_DIM]
        w_ref[0, 0, sl, :] = sol[n][:, GDN_HEAD_DIM:].astype(BF16)
        sd_ref[0, 0, sl, :] = (_mm_nt(q[sl].astype(BF16), k16[n]) * decay[n]).astype(BF16)
        qg_ref[0, 0, sl, :] = (q[sl] * jnp.exp(gc_b[n])).astype(BF16)
        g_last = gc_b[n][ch - 1:ch, :]
        kd_ref[0, 0, sl, :] = (k[sl] * jnp.exp(g_last - gc_b[n])).astype(BF16)
        eg_ref[0, 0, n:n + 1, :] = jnp.exp(g_last)


def _gdn_scan_kernel(u_ref, w_ref, qg_ref, kd_ref, sd_ref, eg_ref, z_ref, nw_ref, s0_ref,
                     o_ref, sout_ref, s_ref, *, tc, ch):
    c = pl.program_id(1)
    hd = GDN_HEAD_DIM

    @pl.when(c == 0)
    def _():
        s_ref[...] = s0_ref[0]

    s = [s_ref[h] for h in range(GDN_HEADS)]
    for n in range(tc // ch):
        sl = slice(n * ch, (n + 1) * ch)
        for h in range(GDN_HEADS):
            s16 = s[h].astype(BF16)
            v_new = u_ref[0, h, sl, :] - _mm(w_ref[0, h, sl, :], s16)
            vn16 = v_new.astype(BF16)
            o = _mm(qg_ref[0, h, sl, :], s16) + _mm(sd_ref[0, h, sl, :], vn16)
            s[h] = eg_ref[0, h, n:n + 1, :] * s[h] + _mm_tn(kd_ref[0, h, sl, :], vn16)
            zg = z_ref[0, sl, h * hd:(h + 1) * hd]
            on = o * lax.rsqrt(jnp.mean(o * o, axis=-1, keepdims=True) + RMS_EPS) * nw_ref[...]
            o_ref[0, sl, h * hd:(h + 1) * hd] = on * (zg * _sigmoid(zg))
    for h in range(GDN_HEADS):
        s_ref[h] = s[h]

    @pl.when(c == pl.num_programs(1) - 1)
    def _():
        for h in range(GDN_HEADS):
            sout_ref[0, h] = s[h]


def _gdn(qkv, z, ab, conv_prev8, s0, conv_w, a_log, dt_bias, norm_w, t_valid):
    b, t, _ = qkv.shape
    tc = min(512, t)
    ch = min(MAX_CHUNK, t)
    nb = t // tc
    nch = t // ch
    hd = GDN_HEAD_DIM
    col = lambda off: pl.BlockSpec((1, tc, hd), lambda bb, h, c: (bb, c, h + off))
    prev = lambda off: pl.BlockSpec((1, 8, hd), lambda bb, h, c: (bb, jnp.maximum(c * (tc // 8) - 1, 0), h + off))
    state = lambda off: pl.BlockSpec((1, 8, hd), lambda bb, h, c: (bb, 0, h + off))
    wspec = lambda off: pl.BlockSpec((GDN_CONV, hd), lambda bb, h, c: (0, h + off))
    per_head = lambda width: pl.BlockSpec((1, 1, tc, width), lambda bb, h, c: (bb, h, c, 0))
    eg_spec = pl.BlockSpec((1, 1, tc // ch, hd), lambda bb, h, c: (bb, h, c, 0))
    smem = pl.BlockSpec(memory_space=pltpu.SMEM)
    offs = (0, GDN_HEADS, 2 * GDN_HEADS)
    u, w, qg, kd, sd, eg = pl.pallas_call(
        functools.partial(_gdn_prep_kernel, tc=tc, ch=ch, t_valid=t_valid),
        grid=(b, GDN_HEADS, nb),
        in_specs=([smem, smem] + [col(o) for o in offs] + [prev(o) for o in offs]
                  + [state(o) for o in offs] + [wspec(o) for o in offs]
                  + [pl.BlockSpec((1, tc, LANES), lambda bb, h, c: (bb, c, 0))]),
        out_specs=[per_head(hd), per_head(hd), per_head(hd), per_head(hd), per_head(ch), eg_spec],
        out_shape=[jax.ShapeDtypeStruct((b, GDN_HEADS, t, hd), F32)]
                  + [jax.ShapeDtypeStruct((b, GDN_HEADS, t, hd), BF16)] * 3
                  + [jax.ShapeDtypeStruct((b, GDN_HEADS, t, ch), BF16),
                     jax.ShapeDtypeStruct((b, GDN_HEADS, nch, hd), F32)],
        scratch_shapes=[pltpu.VMEM((tc + 8, hd), F32)],
        compiler_params=_cparams(("parallel", "parallel", "parallel")),
        name="gdn_prep",
    )(a_log, dt_bias, qkv, qkv, qkv, qkv, qkv, qkv, conv_prev8, conv_prev8, conv_prev8,
      conv_w, conv_w, conv_w, ab)
    all_heads = lambda width: pl.BlockSpec((1, GDN_HEADS, tc, width), lambda bb, c: (bb, 0, c, 0))
    state_spec = pl.BlockSpec((1, GDN_HEADS, hd, hd), lambda bb, c: (bb, 0, 0, 0))
    og, s_out = pl.pallas_call(
        functools.partial(_gdn_scan_kernel, tc=tc, ch=ch),
        grid=(b, nb),
        in_specs=[all_heads(hd), all_heads(hd), all_heads(hd), all_heads(hd), all_heads(ch),
                  pl.BlockSpec((1, GDN_HEADS, tc // ch, hd), lambda bb, c: (bb, 0, c, 0)),
                  pl.BlockSpec((1, tc, GDN_WIDTH), lambda bb, c: (bb, c, 0)),
                  pl.BlockSpec((1, hd), lambda bb, c: (0, 0)),
                  state_spec],
        out_specs=[pl.BlockSpec((1, tc, GDN_WIDTH), lambda bb, c: (bb, c, 0)), state_spec],
        out_shape=[jax.ShapeDtypeStruct((b, t, GDN_WIDTH), F32),
                   jax.ShapeDtypeStruct((b, GDN_HEADS, hd, hd), F32)],
        scratch_shapes=[pltpu.VMEM((GDN_HEADS, hd, hd), F32)],
        compiler_params=_cparams(("parallel", "arbitrary")),
        name="gdn_scan",
    )(u, w, qg, kd, sd, eg, z, norm_w.reshape(1, hd), s0)
    return og, s_out


def _mlstm_kernel(ib_ref, fb_ref, m0_ref, q_ref, k_ref, v_ref, og_ref, if_ref, c0_ref, n0_ref, nw_ref,
                  h_ref, cout_ref, nout_ref, mout_ref, c_ref, n_ref, m_ref, *, tc, ch, t_valid):
    bb = pl.program_id(0)
    p = pl.program_id(1)
    c = pl.program_id(2)
    vd = ML_V_DIM

    @pl.when(c == 0)
    def _():
        c_ref[...] = c0_ref[0, 0]
        n_ref[...] = n0_ref[0, 0]
        for e in range(2):
            m_ref[e] = jnp.full((1, LANES), m0_ref[bb, 2 * p + e], F32)

    lower, _ = _chunk_masks(ch)
    t1 = lower.astype(BF16)
    t2 = (_iota((ch, ch), 0) > _iota((ch, ch), 1)).astype(F32)
    eye = (_iota((ch, ch), 0) == _iota((ch, ch), 1)).astype(F32)
    ones = jnp.ones((ch, ch), BF16)
    lane_row = _iota((1, LANES), 1)
    lane = _iota((tc, LANES), 1)
    gates = if_ref[0]

    def gate_col(idx):
        col = jnp.sum(jnp.where(lane == idx, gates, 0.0), axis=-1, keepdims=True)
        return jnp.broadcast_to(col, (tc, LANES))

    i_b, lf_b = [], []
    for e in range(2):
        hd = 2 * p + e
        i_b.append(gate_col(hd) + ib_ref[hd])
        lf_b.append(-_softplus(-(gate_col(ML_HEADS + hd) + fb_ref[hd])))

    chunks = range(tc // ch)
    sls = [slice(n * ch, (n + 1) * ch) for n in chunks]
    valid = [lower & ((_iota((ch, ch), 1) + (c * tc + n * ch)) < t_valid) for n in chunks]
    row_valid = [(_iota((ch, LANES), 0) + (c * tc + n * ch)) < t_valid for n in chunks]
    v16 = [[v_ref[0, sl, e * vd:(e + 1) * vd].astype(BF16) for sl in sls] for e in range(2)]
    qm, km, qm16, qk, bc_b, dm, ic = ([[None] * len(sls) for _ in range(2)] for _ in range(7))
    for e in range(2):
        in_head = (lane_row >= ML_QK_DIM * e) & (lane_row < ML_QK_DIM * (e + 1))
        for n in chunks:
            qm[e][n] = jnp.where(in_head, q_ref[0, sls[n], :], 0.0)
            km[e][n] = jnp.where(in_head, k_ref[0, sls[n], :] * ML_QK_DIM ** -0.5, 0.0)
            qm16[e][n] = qm[e][n].astype(BF16)
            qk[e][n] = _mm_nt(qm16[e][n], km[e][n].astype(BF16))
            lf = jnp.where(row_valid[n], lf_b[e][sls[n]], 0.0)
            ic[e][n] = i_b[e][sls[n]]
            bc_b[e][n] = _mm_exact_lhs(t1, lf)
            dm[e][n] = (_mm_exact_lhs(t1, lf[:, :ch] * t2)
                        + _mm_exact_lhs(ones, eye * ic[e][n][:, :ch]))

    m_t, m_in = ([[None] * len(sls) for _ in range(2)] for _ in range(2))
    for e in range(2):
        m_prev = m_ref[e]
        for n in chunks:
            m_in[e][n] = m_prev
            row_max = jnp.max(jnp.where(valid[n], dm[e][n], NEG_BIG), axis=-1, keepdims=True)
            m_t[e][n] = jnp.maximum(bc_b[e][n] + m_prev, row_max)
            m_prev = m_t[e][n][ch - 1:ch, :]
        m_ref[e] = m_prev

    s, w_state, a_state, c_upd, n_upd = ([[None] * len(sls) for _ in range(2)] for _ in range(5))
    for e in range(2):
        for n in chunks:
            mt = m_t[e][n]
            w_intra = jnp.where(valid[n], jnp.exp(jnp.where(valid[n], dm[e][n], 0.0) - mt[:, :ch]), 0.0)
            w_state[e][n] = jnp.exp(bc_b[e][n] + m_in[e][n] - mt)
            s[e][n] = qk[e][n] * w_intra
            m_new = mt[ch - 1:ch, :]
            bc_last = bc_b[e][n][ch - 1:ch, :]
            a_state[e][n] = jnp.exp(bc_last + m_in[e][n] - m_new)
            wk = jnp.where(row_valid[n], jnp.exp(bc_last - bc_b[e][n] + ic[e][n] - m_new), 0.0)
            kw = km[e][n] * wk
            c_upd[e][n] = _mm_tn(kw.astype(BF16), v16[e][n])
            n_upd[e][n] = jnp.sum(kw, axis=0, keepdims=True)

    c_state, n_state = [c_ref[...]], [n_ref[...]]
    upper_rows = _iota((LANES, vd), 0) < ML_QK_DIM
    for n in chunks:
        a_rows = jnp.where(upper_rows, a_state[0][n], a_state[1][n])
        c_state.append(a_rows * c_state[n] + (c_upd[0][n] + c_upd[1][n]))
        a_lanes = jnp.where(lane_row < ML_QK_DIM, a_state[0][n], a_state[1][n])
        n_state.append(a_lanes * n_state[n] + (n_upd[0][n] + n_upd[1][n]))
    c_ref[...] = c_state[-1]
    n_ref[...] = n_state[-1]

    for n in chunks:
        cp16 = c_state[n].astype(BF16)
        for e in range(2):
            num = w_state[e][n] * _mm(qm16[e][n], cp16) + _mm(s[e][n].astype(BF16), v16[e][n])
            qn = jnp.sum(qm[e][n] * n_state[n], axis=-1, keepdims=True)
            den = w_state[e][n] * qn + jnp.sum(s[e][n], axis=-1, keepdims=True)
            hh = num / jnp.maximum(jnp.abs(den), jnp.exp(-m_t[e][n]))
            og = og_ref[0, sls[n], e * vd:(e + 1) * vd]
            hn = hh * lax.rsqrt(jnp.mean(hh * hh, axis=-1, keepdims=True) + RMS_EPS) * nw_ref[...]
            h_ref[0, sls[n], e * vd:(e + 1) * vd] = hn * _sigmoid(og)

    @pl.when(c == pl.num_programs(2) - 1)
    def _():
        cout_ref[0, 0] = c_ref[...]
        nout_ref[0, 0] = n_ref[...]
        for e in range(2):
            mout_ref[0, e] = m_ref[e]


def _mlstm(q, k, v, og, ifg, c0, n0, m0, i_bias, f_bias, norm_w, t_valid):
    b, t, _ = q.shape
    tc = min(512, t)
    ch = min(MAX_CHUNK, t)
    nb = t // tc
    pairs = ML_HEADS // 2
    smem = pl.BlockSpec(memory_space=pltpu.SMEM)
    qk_spec = pl.BlockSpec((1, tc, LANES), lambda bb, p, c: (bb, c, p))
    v_spec = pl.BlockSpec((1, tc, 2 * ML_V_DIM), lambda bb, p, c: (bb, c, p))
    c_spec = pl.BlockSpec((1, 1, LANES, ML_V_DIM), lambda bb, p, c: (bb, p, 0, 0))
    n_spec = pl.BlockSpec((1, 1, 1, LANES), lambda bb, p, c: (bb, p, 0, 0))
    m_spec = pl.BlockSpec((1, 2, 1, LANES), lambda bb, p, c: (bb, p, 0, 0))
    h, c_out, n_out, m_out = pl.pallas_call(
        functools.partial(_mlstm_kernel, tc=tc, ch=ch, t_valid=t_valid),
        grid=(b, pairs, nb),
        in_specs=[smem, smem, smem, qk_spec, qk_spec, v_spec, v_spec,
                  pl.BlockSpec((1, tc, LANES), lambda bb, p, c: (bb, c, 0)),
                  c_spec, n_spec, pl.BlockSpec((1, ML_V_DIM), lambda bb, p, c: (0, 0))],
        out_specs=[v_spec, c_spec, n_spec, m_spec],
        out_shape=[jax.ShapeDtypeStruct((b, t, ML_V_WIDTH), F32),
                   jax.ShapeDtypeStruct((b, pairs, LANES, ML_V_DIM), F32),
                   jax.ShapeDtypeStruct((b, pairs, 1, LANES), F32),
                   jax.ShapeDtypeStruct((b, ML_HEADS, 1, LANES), F32)],
        scratch_shapes=[pltpu.VMEM((LANES, ML_V_DIM), F32), pltpu.VMEM((1, LANES), F32),
                        pltpu.VMEM((2, 1, LANES), F32)],
        compiler_params=_cparams(("parallel", "parallel", "arbitrary")),
        name="mlstm",
    )(i_bias, f_bias, m0, q, k, v, og, ifg,
      c0.reshape(b, pairs, LANES, ML_V_DIM), n0.reshape(b, pairs, 1, LANES), norm_w.reshape(1, ML_V_DIM))
    return (h, c_out.reshape(b, ML_HEADS, ML_QK_DIM, ML_V_DIM), n_out.reshape(b, ML_HEADS, ML_QK_DIM),
            m_out[:, :, 0, 0])


GROUP_LANE0 = N_EXPERTS
ROUTE_LANE0 = N_EXPERTS
PAIRS_PER_GROUP = EXPERTS_PER_GROUP * (EXPERTS_PER_GROUP - 1) // 2
N_CLASSES = N_GROUPS * PAIRS_PER_GROUP
MOE_TM = 256
SPARSE_MIN_TOKENS = 4096


def _router_kernel(x_ref, w_ref, b_ref, tri_ref, g_ref, n_ref, *rest):
    xr_ref, cnt_ref = rest[:-1], rest[-1]
    x1, x2, _ = _split3(x_ref[...])
    w1, w2, _ = _split3(w_ref[...])
    logits = _mm(x1, w1) + (_mm(x1, w2) + _mm(x2, w1)) + b_ref[...]
    tm = logits.shape[0]
    lane = _iota((tm, LANES), 1)
    lane_f = lane.astype(F32)
    is_group = (lane >= GROUP_LANE0) & (lane < GROUP_LANE0 + N_GROUPS)
    gl = jnp.where(is_group, logits, NEG_BIG)
    ge = jnp.where(is_group, jnp.exp(gl - jnp.max(gl, axis=-1, keepdims=True)), 0.0)
    gp = ge / jnp.sum(ge, axis=-1, keepdims=True)
    g_gate = jnp.max(gp, axis=-1, keepdims=True)
    g_idx = jnp.min(jnp.where(is_group & (gp == g_gate), lane_f, 1e9), axis=-1, keepdims=True) - GROUP_LANE0
    lo = g_idx * EXPERTS_PER_GROUP
    in_group = (lane_f >= lo) & (lane_f < lo + EXPERTS_PER_GROUP)
    fl = jnp.where(in_group, logits, NEG_BIG)
    fe = jnp.where(in_group, jnp.exp(fl - jnp.max(fl, axis=-1, keepdims=True)), 0.0)
    fp = fe / jnp.sum(fe, axis=-1, keepdims=True)
    w_a = jnp.max(jnp.where(in_group, fp, -1.0), axis=-1, keepdims=True)
    i_a = jnp.min(jnp.where(in_group & (fp == w_a), lane_f, 1e9), axis=-1, keepdims=True)
    rest = in_group & (lane_f != i_a)
    w_b = jnp.max(jnp.where(rest, fp, -1.0), axis=-1, keepdims=True)
    i_b = jnp.min(jnp.where(rest & (fp == w_b), lane_f, 1e9), axis=-1, keepdims=True)
    tot = w_a + w_b
    gate_a = g_gate * (w_a / tot)
    gate_b = g_gate * (w_b / tot)
    e_lo = jnp.minimum(i_a, i_b)
    e_hi = jnp.maximum(i_a, i_b)
    a_loc = e_lo - lo
    pair = a_loc * (7.0 - a_loc) * 0.5 + (e_hi - e_lo - 1.0)
    cls = g_idx * PAIRS_PER_GROUP + pair
    @pl.when(pl.program_id(0) == 0)
    def _():
        cnt_ref[...] = jnp.zeros_like(cnt_ref)

    onehot = jnp.where(lane_f == cls, 1.0, 0.0)
    earlier = _mm(tri_ref[...], onehot.astype(BF16)) + cnt_ref[0:1, :]
    rank = jnp.sum(onehot * earlier, axis=-1, keepdims=True)
    cnt_ref[...] = cnt_ref[...] + jnp.sum(onehot, axis=0, keepdims=True)
    n_ref[...] = cnt_ref[...]
    record = (e_lo, e_hi, jnp.where(i_a < i_b, gate_a, gate_b), jnp.where(i_a < i_b, gate_b, gate_a), cls, rank)
    out = jnp.where(lane_f == i_a, gate_a, jnp.where(lane_f == i_b, gate_b, 0.0))
    for off, val in enumerate(record):
        out = jnp.where(lane == ROUTE_LANE0 + off, val, out)
    g_ref[...] = out
    if xr_ref:
        d = x_ref.shape[1]
        xr_ref[0][:, :d] = x_ref[...]
        xr_ref[0][:, d:] = out


def _router(x, w_group, b_group, w_fine, b_fine, tm, with_rows):
    n, d = x.shape
    tm = min(tm, n)
    w = jnp.concatenate([w_fine.reshape(d, N_EXPERTS), w_group], axis=1)
    w = jnp.pad(w, ((0, 0), (0, LANES - w.shape[1])))
    bias = jnp.concatenate([b_fine.reshape(N_EXPERTS), b_group])
    bias = jnp.pad(bias, (0, LANES - bias.shape[0])).reshape(1, LANES)
    tri = (jnp.arange(tm)[:, None] > jnp.arange(tm)[None, :]).astype(BF16)
    out_specs = [pl.BlockSpec((tm, LANES), lambda i: (i, 0)), pl.BlockSpec((8, LANES), lambda i: (0, 0))]
    out_shape = [jax.ShapeDtypeStruct((n, LANES), F32), jax.ShapeDtypeStruct((8, LANES), F32)]
    if with_rows:
        out_specs.append(pl.BlockSpec((tm, d + LANES), lambda i: (i, 0)))
        out_shape.append(jax.ShapeDtypeStruct((n, d + LANES), F32))
    return pl.pallas_call(
        _router_kernel,
        grid=(n // tm,),
        in_specs=[pl.BlockSpec((tm, d), lambda i: (i, 0)),
                  pl.BlockSpec((d, LANES), lambda i: (0, 0)),
                  pl.BlockSpec((1, LANES), lambda i: (0, 0)),
                  pl.BlockSpec((tm, tm), lambda i: (0, 0))],
        out_specs=out_specs,
        out_shape=out_shape,
        scratch_shapes=[pltpu.VMEM((8, LANES), F32)],
        compiler_params=_cparams(("arbitrary",)),
        name="router",
    )(x, w, bias, tri)


def _moe_kernel(x_ref, gt_ref, wg_ref, wu_ref, wd_ref, g_ref, b_ref, o_ref, xb_ref, acc_ref):
    e = pl.program_id(1)

    @pl.when(e == 0)
    def _():
        xb_ref[...] = x_ref[...].astype(BF16)
        acc_ref[...] = jnp.zeros_like(acc_ref)

    gates = gt_ref[...]
    lane = _iota(gates.shape, 1)
    gcol = jnp.sum(jnp.where(lane == e, gates, 0.0), axis=-1, keepdims=True)
    xb = xb_ref[...]
    hg = _mm(xb, wg_ref[0])
    hu = _mm(xb, wu_ref[0])
    hh = (hg * _sigmoid(hg)) * hu * gcol
    acc_ref[...] += _mm(hh.astype(BF16), wd_ref[0])

    @pl.when(e == pl.num_programs(1) - 1)
    def _():
        o_ref[...] = _layer_norm_rows(DN_ALPHA * x_ref[...] + acc_ref[...], g_ref[...], b_ref[...])


def _moe_ln(x, gates, wg, wu, wd, g, b, tm):
    n, d = x.shape
    tm = min(tm, n)
    f = wg.shape[2]
    return pl.pallas_call(
        _moe_kernel,
        grid=(n // tm, N_EXPERTS),
        in_specs=[pl.BlockSpec((tm, d), lambda i, e: (i, 0)),
                  pl.BlockSpec((tm, LANES), lambda i, e: (i, 0)),
                  pl.BlockSpec((1, d, f), lambda i, e: (e, 0, 0)),
                  pl.BlockSpec((1, d, f), lambda i, e: (e, 0, 0)),
                  pl.BlockSpec((1, f, d), lambda i, e: (e, 0, 0)),
                  pl.BlockSpec((1, d), lambda i, e: (0, 0)),
                  pl.BlockSpec((1, d), lambda i, e: (0, 0))],
        out_specs=pl.BlockSpec((tm, d), lambda i, e: (i, 0)),
        out_shape=jax.ShapeDtypeStruct((n, d), F32),
        scratch_shapes=[pltpu.VMEM((tm, d), BF16), pltpu.VMEM((tm, d), F32)],
        compiler_params=_cparams(("parallel", "arbitrary")),
        name="moe_ln",
    )(x, gates, wg, wu, wd, g.reshape(1, d), b.reshape(1, d))


def _route_plan(cls, rank, counts, n, tm):
    n_tiles = (n + N_CLASSES * tm) // tm
    pairs = [(a, b) for a in range(EXPERTS_PER_GROUP) for b in range(a + 1, EXPERTS_PER_GROUP)]
    lo_tab = jnp.asarray([EXPERTS_PER_GROUP * g + a for g in range(N_GROUPS) for a, _ in pairs], jnp.int32)
    hi_tab = jnp.asarray([EXPERTS_PER_GROUP * g + b for g in range(N_GROUPS) for _, b in pairs], jnp.int32)
    classes = jnp.arange(N_CLASSES, dtype=jnp.int32)
    padded = (counts + tm - 1) // tm * tm
    p_end = jnp.sum(jnp.where(classes[None, :] <= classes[:, None], padded[None, :], 0), axis=1)
    p_start = p_end - padded
    pos = jnp.sum(jnp.where(cls[:, None] == classes[None, :], p_start[None, :], 0), axis=1) + rank
    last_tile_row = jnp.where(padded > 0, p_end - tm, -1)
    tile_row0 = jnp.arange(n_tiles, dtype=jnp.int32) * tm
    tile_cls = jnp.minimum(jnp.sum((p_end[None, :] <= tile_row0[:, None]).astype(jnp.int32), axis=1), N_CLASSES - 1)
    pick = lambda tab: jnp.sum(jnp.where(tile_cls[:, None] == classes[None, :], tab[None, :], 0), axis=1)
    n_used = (p_end[N_CLASSES - 1] // tm).reshape(1)
    return pos.astype(jnp.int32), last_tile_row.astype(jnp.int32), pick(lo_tab), pick(hi_tab), n_used.astype(jnp.int32)


def _start_row_copies(src_row, dst_row, sems, tm):
    def issue(r2, carry):
        for prio in range(2):
            r = 2 * r2 + prio
            pltpu.make_async_copy(src_row(r), dst_row(r), sems.at[prio]).start(priority=prio)
        return carry
    lax.fori_loop(0, tm // 2, issue, 0, unroll=4)


def _wait_row_copies(src_row, dst_row, sems, tm):
    for prio in range(2):
        pltpu.make_async_copy(src_row(0, tm // 2), dst_row(0, tm // 2), sems.at[prio]).wait()


def _row_copies(src_row, dst_row, sems, tm):
    _start_row_copies(src_row, dst_row, sems, tm)
    _wait_row_copies(src_row, dst_row, sems, tm)


def _dispatch_rows_kernel(pos_ref, last_ref, nused_ref, xr_ref, xs_hbm, zero_ref, stage_ref, sems,
                          *, tm, n_tiles):
    i = pl.program_id(0)
    last = pl.num_programs(0) - 1

    @pl.when(i == 0)
    def _():
        zero_ref[...] = jnp.zeros_like(zero_ref)

        def fill(row0):
            cp = pltpu.make_async_copy(zero_ref, xs_hbm.at[pl.ds(pl.multiple_of(row0, tm), tm)], sems.at[0, 0])
            cp.start()
            cp.wait()

        for c in range(N_CLASSES):
            @pl.when(last_ref[c] >= 0)
            def _():
                fill(last_ref[c])

        def fill_unused(t, carry):
            fill(t * tm)
            return carry
        lax.fori_loop(nused_ref[0], n_tiles, fill_unused, 0)

    def copies(tile, slot):
        return (lambda r, rows=1: stage_ref.at[slot, pl.ds(r, rows)],
                lambda r, rows=1: xs_hbm.at[pl.ds(pos_ref[tile * tm + r] if rows == 1 else 0, rows)],
                sems.at[slot], tm)

    slot = i % 2

    @pl.when(i >= 2)
    def _():
        _wait_row_copies(*copies(i - 2, slot))

    stage_ref[slot] = xr_ref[...]
    _start_row_copies(*copies(i, slot))

    @pl.when(i == last)
    def _():
        _wait_row_copies(*copies(i - 1, 1 - slot))
        _wait_row_copies(*copies(i, slot))


def _combine_ln_kernel(pos_ref, ys_hbm, x_ref, g_ref, b_ref, o_ref, buf_ref, sems, *, tm):
    i = pl.program_id(0)
    last = pl.num_programs(0) - 1

    def copies(tile, slot):
        return (lambda r, rows=1: ys_hbm.at[pl.ds(pos_ref[tile * tm + r] if rows == 1 else 0, rows)],
                lambda r, rows=1: buf_ref.at[slot, pl.ds(r, rows)], sems.at[slot], tm)

    @pl.when(i == 0)
    def _():
        _start_row_copies(*copies(0, 0))

    @pl.when(i < last)
    def _():
        _start_row_copies(*copies(i + 1, (i + 1) % 2))

    slot = i % 2
    _wait_row_copies(*copies(i, slot))
    o_ref[...] = _layer_norm_rows(DN_ALPHA * x_ref[...] + buf_ref[slot], g_ref[...], b_ref[...])


def _expert_pair_kernel(lo_ref, hi_ref, nused_ref, xs_ref, wg_lo, wu_lo, wd_lo, wg_hi, wu_hi, wd_hi, y_ref, *, d):
    i = pl.program_id(0)

    @pl.when(i < nused_ref[0])
    def _():
        x16 = xs_ref[:, :d].astype(BF16)
        y = jnp.zeros(y_ref.shape, F32)
        for off, (wg, wu, wd) in ((2, (wg_lo, wu_lo, wd_lo)), (3, (wg_hi, wu_hi, wd_hi))):
            gate = xs_ref[:, d + ROUTE_LANE0 + off:d + ROUTE_LANE0 + off + 1]
            hg = _mm(x16, wg[0])
            hu = _mm(x16, wu[0])
            y = y + _mm(((hg * _sigmoid(hg)) * hu * gate).astype(BF16), wd[0])
        y_ref[...] = y

    @pl.when(i >= nused_ref[0])
    def _():
        y_ref[...] = jnp.zeros_like(y_ref)


def _moe_ln_sparse(x, gates, counts, xr, wg, wu, wd, g, b):
    n, d = x.shape
    tm = MOE_TM
    f = wg.shape[2]
    record = gates[:, ROUTE_LANE0 + 4:ROUTE_LANE0 + 6].astype(jnp.int32)
    pos, last_tile_row, tile_lo, tile_hi, n_used = _route_plan(
        record[:, 0], record[:, 1], counts[0, :N_CLASSES].astype(jnp.int32), n, tm)
    n_tiles = (n + N_CLASSES * tm) // tm
    n_rows = n_tiles * tm
    assert n // tm >= 2
    xs = pl.pallas_call(
        functools.partial(_dispatch_rows_kernel, tm=tm, n_tiles=n_tiles),
        grid_spec=pltpu.PrefetchScalarGridSpec(
            num_scalar_prefetch=3, grid=(n // tm,),
            in_specs=[pl.BlockSpec((tm, d + LANES), lambda i, ps, la, u: (i, 0))],
            out_specs=pl.BlockSpec(memory_space=pl.ANY),
            scratch_shapes=[pltpu.VMEM((tm, d + LANES), F32), pltpu.VMEM((2, tm, d + LANES), F32),
                            pltpu.SemaphoreType.DMA((2, 2))]),
        out_shape=jax.ShapeDtypeStruct((n_rows, d + LANES), F32),
        compiler_params=_cparams(("arbitrary",)),
        name="moe_dispatch",
    )(pos, last_tile_row, n_used, xr)
    wspec = lambda which, shape: pl.BlockSpec(
        (1,) + shape, (lambda i, lo, hi, u: (lo[i], 0, 0)) if which == 0 else (lambda i, lo, hi, u: (hi[i], 0, 0)))
    ys = pl.pallas_call(
        functools.partial(_expert_pair_kernel, d=d),
        grid_spec=pltpu.PrefetchScalarGridSpec(
            num_scalar_prefetch=3, grid=(n_tiles,),
            in_specs=[pl.BlockSpec((tm, d + LANES), lambda i, lo, hi, u: (i, 0)),
                      wspec(0, (d, f)), wspec(0, (d, f)), wspec(0, (f, d)),
                      wspec(1, (d, f)), wspec(1, (d, f)), wspec(1, (f, d))],
            out_specs=pl.BlockSpec((tm, d), lambda i, lo, hi, u: (i, 0))),
        out_shape=jax.ShapeDtypeStruct((n_rows, d), F32),
        compiler_params=_cparams(("arbitrary",)),
        name="moe_experts",
    )(tile_lo, tile_hi, n_used, xs, wg, wu, wd, wg, wu, wd)
    return pl.pallas_call(
        functools.partial(_combine_ln_kernel, tm=tm),
        grid_spec=pltpu.PrefetchScalarGridSpec(
            num_scalar_prefetch=1, grid=(n // tm,),
            in_specs=[pl.BlockSpec(memory_space=pl.ANY),
                      pl.BlockSpec((tm, d), lambda i, ps: (i, 0)),
                      pl.BlockSpec((1, d), lambda i, ps: (0, 0)),
                      pl.BlockSpec((1, d), lambda i, ps: (0, 0))],
            out_specs=pl.BlockSpec((tm, d), lambda i, ps: (i, 0)),
            scratch_shapes=[pltpu.VMEM((2, tm, d), F32), pltpu.SemaphoreType.DMA((2, 2))]),
        out_shape=jax.ShapeDtypeStruct((n, d), F32),
        compiler_params=_cparams(("arbitrary",)),
        name="moe_combine_ln",
    )(pos, ys, x, g.reshape(1, d), b.reshape(1, d))


EVEN_WIDTHS = (SB_WIDTH, SB_WIDTH, SB_WIDTH, 3 * GDN_WIDTH, GDN_WIDTH, LANES)
ODD_WIDTHS = (ML_QK_WIDTH, ML_QK_WIDTH, ML_V_WIDTH, ML_V_WIDTH, LANES)
ROW_TILE = 512


def _pad_cols(w, total):
    return jnp.pad(w, ((0, 0), (0, total - w.shape[1])))


def _pad_tokens(a, t_pad):
    return jnp.pad(a, ((0, 0), (0, t_pad - a.shape[1])) + ((0, 0),) * (a.ndim - 2))


def _run_trunk(x, past, conv_buf, gdn_state, ml_state, p):
    b, t, d = x.shape
    n = b * t
    t_pad = max(t, MIN_CHUNK)
    xt = x.reshape(n, d)

    q_sb, k_sb, v_sb, qkv_g, z_g, ab = _project(xt, p['even_w_in'], EVEN_WIDTHS, ROW_TILE)
    q3, k3, v3 = (a.reshape(b, t, SB_WIDTH) for a in (q_sb, k_sb, v_sb))
    if past is None:
        o_sb = _sb_prompt(q3, k3, v3, p['sb_bias'])
    else:
        o_sb = _sb_sample(q3, k3, v3, past[0], past[1], past[2], p['sb_bias'])
    qkv3 = qkv_g.reshape(b, t, 3 * GDN_WIDTH)
    conv_prev8 = jnp.pad(conv_buf, ((0, 0), (8 - (GDN_CONV - 1), 0), (0, 0)))
    o_g, s_out = _gdn(_pad_tokens(qkv3, t_pad), _pad_tokens(z_g.reshape(b, t, GDN_WIDTH), t_pad),
                      _pad_tokens(ab.reshape(b, t, LANES), t_pad), conv_prev8, gdn_state,
                      p['gdn_conv_w'], p['gdn_a_log'], p['gdn_dt_bias'], p['gdn_norm_w'], t)
    new_buf = jnp.concatenate([conv_buf, qkv3], axis=1)[:, -(GDN_CONV - 1):]
    merged = jnp.concatenate([o_sb, o_g[:, :t]], axis=-1).reshape(n, SB_WIDTH + GDN_WIDTH)
    xt = _outproj_ln(merged, p['even_w_out'], xt, p['ln_mix_g'][0], p['ln_mix_b'][0], ROW_TILE)
    xt = _ffn(xt, p, 0)

    q_m, k_m, v_m, o_m, ifg = _project(xt, p['odd_w_in'], ODD_WIDTHS, ROW_TILE)
    c0, n0, m0 = ml_state
    h_m, c_out, n_out, m_out = _mlstm(
        _pad_tokens(q_m.reshape(b, t, ML_QK_WIDTH), t_pad), _pad_tokens(k_m.reshape(b, t, ML_QK_WIDTH), t_pad),
        _pad_tokens(v_m.reshape(b, t, ML_V_WIDTH), t_pad), _pad_tokens(o_m.reshape(b, t, ML_V_WIDTH), t_pad),
        _pad_tokens(ifg.reshape(b, t, LANES), t_pad), c0, n0, m0,
        p['mlstm_i_bias'], p['mlstm_f_bias'], p['mlstm_norm_w'], t)
    xt = _outproj_ln(h_m[:, :t].reshape(n, ML_V_WIDTH), p['odd_w_out'], xt,
                     p['ln_mix_g'][1], p['ln_mix_b'][1], ROW_TILE)
    xt = _ffn(xt, p, 1)

    kv_shape = (1, b, t, SB_HEADS, SB_HEAD_DIM)
    return (xt.reshape(b, t, d), k_sb.reshape(kv_shape), v_sb.reshape(kv_shape), new_buf[None], s_out[None],
            c_out[None], n_out[None], m_out[None])


def _ffn(xt, p, layer):
    sparse = xt.shape[0] >= SPARSE_MIN_TOKENS
    routed = _router(xt, p['moe_w_group'][layer], p['moe_b_group'][layer], p['moe_w_fine'][layer],
                     p['moe_b_fine'][layer], ROW_TILE, sparse)
    weights = (p['moe_w_gate'][layer], p['moe_w_up'][layer], p['moe_w_down'][layer],
               p['ln_ffn_g'][layer], p['ln_ffn_b'][layer])
    if sparse:
        return _moe_ln_sparse(xt, routed[0], routed[1], routed[2], *weights)
    return _moe_ln(xt, routed[0], *weights, ROW_TILE)


def kernel(x_prompt, x_sample, cache_k, cache_v, state_conv, state_gdn, state_mlstm_C, state_mlstm_n,
           state_mlstm_m, page_table, even_w_in, even_w_out, sb_bias, gdn_conv_w, gdn_a_log, gdn_dt_bias,
           gdn_norm_w, odd_w_in, odd_w_out, mlstm_i_bias, mlstm_f_bias, mlstm_norm_w, ln_mix_g, ln_mix_b,
           ln_ffn_g, ln_ffn_b, moe_w_group, moe_b_group, moe_w_fine, moe_b_fine, moe_w_gate, moe_w_up,
           moe_w_down):
    assert DEPTH == 2 and even_w_in.shape[0] == 1 and odd_w_in.shape[0] == 1
    p = {
        'even_w_in': _pad_cols(even_w_in[0], sum(EVEN_WIDTHS)).astype(BF16),
        'even_w_out': even_w_out[0].astype(BF16),
        'sb_bias': sb_bias[0], 'gdn_conv_w': gdn_conv_w[0], 'gdn_a_log': gdn_a_log[0],
        'gdn_dt_bias': gdn_dt_bias[0], 'gdn_norm_w': gdn_norm_w[0],
        'odd_w_in': _pad_cols(odd_w_in[0], sum(ODD_WIDTHS)).astype(BF16),
        'odd_w_out': odd_w_out[0].astype(BF16),
        'mlstm_i_bias': mlstm_i_bias[0], 'mlstm_f_bias': mlstm_f_bias[0], 'mlstm_norm_w': mlstm_norm_w[0],
        'ln_mix_g': ln_mix_g, 'ln_mix_b': ln_mix_b, 'ln_ffn_g': ln_ffn_g, 'ln_ffn_b': ln_ffn_b,
        'moe_w_group': moe_w_group, 'moe_b_group': moe_b_group, 'moe_w_fine': moe_w_fine,
        'moe_b_fine': moe_b_fine, 'moe_w_gate': moe_w_gate.astype(BF16), 'moe_w_up': moe_w_up.astype(BF16),
        'moe_w_down': moe_w_down.astype(BF16),
    }
    bp = x_prompt.shape[0]
    zero_buf = jnp.zeros((bp, GDN_CONV - 1, 3 * GDN_WIDTH), F32)
    zero_s = jnp.zeros((bp, GDN_HEADS, GDN_HEAD_DIM, GDN_HEAD_DIM), F32)
    zero_ml = (jnp.zeros((bp, ML_HEADS, ML_QK_DIM, ML_V_DIM), F32), jnp.zeros((bp, ML_HEADS, ML_QK_DIM), F32),
               jnp.zeros((bp, ML_HEADS), F32))
    (y_p, k_p, v_p, conv_p, gdn_p, mc_p, mn_p, mm_p) = _run_trunk(x_prompt, None, zero_buf, zero_s, zero_ml, p)

    pool_k = jnp.transpose(cache_k.reshape(cache_k.shape[1:]), (0, 2, 3, 1))
    pool_v = jnp.transpose(cache_v.reshape(cache_v.shape[1:]), (0, 2, 3, 1))
    (y_s, k_s, v_s, conv_s, gdn_s, mc_s, mn_s, mm_s) = _run_trunk(
        x_sample, (pool_k, pool_v, page_table), state_conv[0], state_gdn[0],
        (state_mlstm_C[0], state_mlstm_n[0], state_mlstm_m[0]), p)
    return (y_p, y_s, k_p, v_p, k_s, v_s, conv_p, conv_s, gdn_p, gdn_s, mc_p, mc_s, mn_p, mn_s, mm_p, mm_s)
```
